```python
import math
import jax, jax.numpy as jnp
from jax import lax
import numpy as np

D_MODEL = 2048
BATCH = 1
SEQ = 8192
DEPTH = 4

N_ATT_HEADS = 8
HEAD_DIM = 128
D_ATT = N_ATT_HEADS * HEAD_DIM
DILATED_BRANCHES = ((128, 1), (512, 4), (2048, 16))
BAND_BLOCK = 128
ALIBI_MAX_BIAS = 8.0
D_CONV = 1024
CONV_WIDTH = 31
D_IN_EVEN = 3 * D_ATT + 2 * D_CONV
D_SSM = 1024
SSM_GROUP = 16
N_SSM_GROUPS = D_SSM // SSM_GROUP
SSM_STATE = 64
D_FF = 7168
N_EXPERTS = 8
TOP_K = 2
NORM_EPS = 1e-5

kernel_name = "hybrid_dilated_conv_s5_moe_trunk"


def rmsnorm(x, g):
    xf = x.astype(jnp.float32)
    y = xf * lax.rsqrt(jnp.mean(xf * xf, axis=-1, keepdims=True) + NORM_EPS)
    return (y * g.astype(jnp.float32)).astype(x.dtype)


def layernorm(x, g, b):
    xf = x.astype(jnp.float32)
    mu = jnp.mean(xf, axis=-1, keepdims=True)
    var = jnp.mean(jnp.square(xf - mu), axis=-1, keepdims=True)
    y = (xf - mu) * lax.rsqrt(var + NORM_EPS) * g.astype(jnp.float32) + b.astype(jnp.float32)
    return y.astype(x.dtype)


def swiglu(h, w1, w3, w2):
    return (jax.nn.silu(h @ w1) * (h @ w3)) @ w2


def alibi_slopes(n_heads):
    h = jnp.arange(1, n_heads + 1, dtype=jnp.float32)
    return jnp.exp2(-ALIBI_MAX_BIAS * h / n_heads)


def dilated_branch(q, k, v, slopes, window, dilation):
    b, s_len, h, hd = q.shape
    span = window // dilation
    sub_len = s_len // dilation
    nb = -(-sub_len // BAND_BLOCK)
    pad = nb * BAND_BLOCK - sub_len

    def to_sub(t):
        t = t.reshape(b, sub_len, dilation, h, hd).transpose(0, 2, 3, 1, 4)
        return jnp.pad(t, ((0, 0), (0, 0), (0, 0), (0, pad), (0, 0)))

    def banded(t):
        t = jnp.pad(t, ((0, 0), (0, 0), (0, 0), (BAND_BLOCK, 0), (0, 0)))
        t = t.reshape(b, dilation, h, nb + 1, BAND_BLOCK, hd)
        return jnp.concatenate([t[:, :, :, :-1], t[:, :, :, 1:]], axis=4)

    qb = to_sub(q).reshape(b, dilation, h, nb, BAND_BLOCK, hd)
    kb = banded(to_sub(k))
    vb = banded(to_sub(v))
    scores = jnp.einsum('brhnqd,brhnkd->brhnqk', qb, kb,
                        preferred_element_type=jnp.float32) * (HEAD_DIM ** -0.5)
    qi = jnp.arange(BAND_BLOCK)[:, None]
    kj = jnp.arange(2 * BAND_BLOCK)[None, :]
    dist = BAND_BLOCK + qi - kj
    kpos = jnp.arange(nb)[:, None] * BAND_BLOCK - BAND_BLOCK + jnp.arange(2 * BAND_BLOCK)[None, :]
    valid = (dist >= 0) & (dist <= span) & (kpos >= 0)[:, None, :]
    alibi = -slopes[:, None, None, None] * (dist * dilation).astype(jnp.float32)
    scores = jnp.where(valid, scores + alibi, -jnp.inf)
    m = jnp.max(scores, axis=-1, keepdims=True)
    p = jnp.exp(scores - m)
    den = jnp.sum(p, axis=-1, keepdims=True)
    o = jnp.einsum('brhnqk,brhnkd->brhnqd', p, vb.astype(jnp.float32)) / den
    lse = (m + jnp.log(den))[..., 0]
    o = o.reshape(b, dilation, h, nb * BAND_BLOCK, hd)[:, :, :, :sub_len]
    o = o.transpose(0, 3, 1, 2, 4).reshape(b, s_len, h, hd)
    lse = lse.reshape(b, dilation, h, nb * BAND_BLOCK)[:, :, :, :sub_len]
    lse = lse.transpose(0, 3, 1, 2).reshape(b, s_len, h)
    return o, lse


def even_mixer(h, w_in, conv_w, conv_b, cln_g, cln_b, w_out):
    b, s_len, _ = h.shape
    proj = h @ w_in
    q, k, v, c_val, c_gate = jnp.split(
        proj, [D_ATT, 2 * D_ATT, 3 * D_ATT, 3 * D_ATT + D_CONV], axis=-1)
    q = q.reshape(b, s_len, N_ATT_HEADS, HEAD_DIM)
    k = k.reshape(b, s_len, N_ATT_HEADS, HEAD_DIM)
    v = v.reshape(b, s_len, N_ATT_HEADS, HEAD_DIM)
    slopes = alibi_slopes(N_ATT_HEADS)
    outs, lses = [], []
    for window, dilation in DILATED_BRANCHES:
        o, l = dilated_branch(q, k, v, slopes, window, dilation)
        outs.append(o)
        lses.append(l)
    alpha = jax.nn.softmax(jnp.stack(lses, axis=0), axis=0)
    att = jnp.einsum('gbsh,gbshd->bshd', alpha, jnp.stack(outs, axis=0))
    att = att.reshape(b, s_len, D_ATT).astype(h.dtype)
    c = c_val * jax.nn.sigmoid(c_gate)
    c = lax.conv_general_dilated(
        c, conv_w.reshape(CONV_WIDTH, 1, D_CONV), window_strides=(1,),
        padding=[(CONV_WIDTH - 1, 0)], dimension_numbers=('NWC', 'WIO', 'NWC'),
        feature_group_count=D_CONV) + conv_b
    c = jax.nn.silu(layernorm(c, cln_g, cln_b))
    return jnp.concatenate([att, c], axis=-1) @ w_out


def _diag_combine(e1, e2):
    a1r, a1i, b1r, b1i = e1
    a2r, a2i, b2r, b2i = e2
    ar = a2r * a1r - a2i * a1i
    ai = a2r * a1i + a2i * a1r
    br = a2r * b1r - a2i * b1i + b2r
    bi = a2r * b1i + a2i * b1r + b2i
    return ar, ai, br, bi


def ssm_mixer(h, w_in, lam_re, lam_im, log_dt, b_re, b_im, c_re, c_im, d_skip, glu_wa, glu_wg):
    b, s_len, _ = h.shape
    u = (h @ w_in).astype(jnp.float32).reshape(b, s_len, N_SSM_GROUPS, SSM_GROUP)
    lr = lam_re.astype(jnp.float32)
    li = lam_im.astype(jnp.float32)
    dt = jnp.exp(log_dt.astype(jnp.float32))[:, None]
    mag = jnp.exp(lr * dt)
    abar_re = mag * jnp.cos(li * dt)
    abar_im = mag * jnp.sin(li * dt)
    inv = 1.0 / (lr * lr + li * li)
    f_re = ((abar_re - 1.0) * lr + abar_im * li) * inv
    f_im = (abar_im * lr - (abar_re - 1.0) * li) * inv
    br = b_re.astype(jnp.float32)
    bi = b_im.astype(jnp.float32)
    bbar_re = f_re[..., None] * br - f_im[..., None] * bi
    bbar_im = f_re[..., None] * bi + f_im[..., None] * br
    bu_re = jnp.einsum('bsgc,gpc->bsgp', u, bbar_re)
    bu_im = jnp.einsum('bsgc,gpc->bsgp', u, bbar_im)
    a_re = jnp.broadcast_to(abar_re, bu_re.shape)
    a_im = jnp.broadcast_to(abar_im, bu_im.shape)
    _, _, x_re, x_im = lax.associative_scan(_diag_combine, (a_re, a_im, bu_re, bu_im), axis=1)
    y = (jnp.einsum('bsgp,gcp->bsgc', x_re, c_re.astype(jnp.float32))
         - jnp.einsum('bsgp,gcp->bsgc', x_im, c_im.astype(jnp.float32))
         + d_skip.astype(jnp.float32).reshape(N_SSM_GROUPS, SSM_GROUP) * u)
    y = jax.nn.gelu(y.reshape(b, s_len, D_SSM)).astype(h.dtype)
    return (y @ glu_wa) * jax.nn.sigmoid(y @ glu_wg)


def moe_ffn(h, router, w1, w3, w2):
    logits = jnp.einsum('bsd,de->bse', h, router, preferred_element_type=jnp.float32)
    top_v, top_i = lax.top_k(logits, TOP_K)
    gates = jax.nn.softmax(top_v, axis=-1)
    weights = jnp.einsum('bsk,bske->bse', gates,
                         jax.nn.one_hot(top_i, N_EXPERTS, dtype=jnp.float32))
    out = jnp.zeros(h.shape, jnp.float32)
    for e in range(N_EXPERTS):
        out = out + weights[..., e:e + 1] * swiglu(h, w1[e], w3[e], w2[e])
    return out.astype(h.dtype)


def setup_inputs(seed: int = 0) -> dict:
    key = jax.random.key(seed)
    ks = iter(jax.random.split(key, 64))
    n_even = (DEPTH + 1) // 2
    n_odd = DEPTH // 2
    f32 = jnp.float32

    def nrm(shape, scale):
        return jax.random.normal(next(ks), shape, f32) * scale

    def gain(shape):
        return 1.0 + 0.02 * jax.random.normal(next(ks), shape, f32)

    inp = {}
    inp['x'] = jax.random.normal(next(ks), (BATCH, SEQ, D_MODEL), f32)
    inp['even_norm1'] = gain((n_even, D_MODEL))
    inp['even_w_in'] = nrm((n_even, D_MODEL, D_IN_EVEN), D_MODEL ** -0.5)
    inp['even_conv_w'] = nrm((n_even, CONV_WIDTH, D_CONV), CONV_WIDTH ** -0.5)
    inp['even_conv_b'] = nrm((n_even, D_CONV), 0.01)
    inp['even_cln_g'] = gain((n_even, D_CONV))
    inp['even_cln_b'] = nrm((n_even, D_CONV), 0.01)
    inp['even_w_out'] = nrm((n_even, D_ATT + D_CONV, D_MODEL), (D_ATT + D_CONV) ** -0.5)
    inp['even_norm2'] = gain((n_even, D_MODEL))
    inp['even_ffn_w1'] = nrm((n_even, D_MODEL, D_FF), D_MODEL ** -0.5)
    inp['even_ffn_w3'] = nrm((n_even, D_MODEL, D_FF), D_MODEL ** -0.5)
    inp['even_ffn_w2'] = nrm((n_even, D_FF, D_MODEL), D_FF ** -0.5)
    inp['odd_norm1'] = gain((n_odd, D_MODEL))
    inp['odd_ssm_w_in'] = nrm((n_odd, D_MODEL, D_SSM), D_MODEL ** -0.5)
    inp['odd_lam_re'] = -0.5 + nrm((n_odd, N_SSM_GROUPS, SSM_STATE), 0.01)
    inp['odd_lam_im'] = (math.pi * jnp.arange(SSM_STATE, dtype=f32)
                         + nrm((n_odd, N_SSM_GROUPS, SSM_STATE), 0.01))
    inp['odd_log_dt'] = jax.random.uniform(next(ks), (n_odd, N_SSM_GROUPS), f32,
                                           math.log(1e-3), math.log(1e-1))
    inp['odd_b_re'] = nrm((n_odd, N_SSM_GROUPS, SSM_STATE, SSM_GROUP), (2 * SSM_GROUP) ** -0.5)
    inp['odd_b_im'] = nrm((n_odd, N_SSM_GROUPS, SSM_STATE, SSM_GROUP), (2 * SSM_GROUP) ** -0.5)
    inp['odd_c_re'] = nrm((n_odd, N_SSM_GROUPS, SSM_GROUP, SSM_STATE), SSM_STATE ** -0.5)
    inp['odd_c_im'] = nrm((n_odd, N_SSM_GROUPS, SSM_GROUP, SSM_STATE), SSM_STATE ** -0.5)
    inp['odd_d_skip'] = nrm((n_odd, D_SSM), 1.0)
    inp['odd_glu_wa'] = nrm((n_odd, D_SSM, D_MODEL), D_SSM ** -0.5)
    inp['odd_glu_wg'] = nrm((n_odd, D_SSM, D_MODEL), D_SSM ** -0.5)
    inp['odd_norm2'] = gain((n_odd, D_MODEL))
    inp['odd_router'] = nrm((n_odd, D_MODEL, N_EXPERTS), D_MODEL ** -0.5)
    inp['odd_moe_w1'] = nrm((n_odd, N_EXPERTS, D_MODEL, D_FF), D_MODEL ** -0.5)
    inp['odd_moe_w3'] = nrm((n_odd, N_EXPERTS, D_MODEL, D_FF), D_MODEL ** -0.5)
    inp['odd_moe_w2'] = nrm((n_odd, N_EXPERTS, D_FF, D_MODEL), D_FF ** -0.5)
    inp['final_norm'] = gain((D_MODEL,))
    return inp


def reference(x, even_norm1, even_w_in, even_conv_w, even_conv_b, even_cln_g, even_cln_b,
              even_w_out, even_norm2, even_ffn_w1, even_ffn_w3, even_ffn_w2,
              odd_norm1, odd_ssm_w_in, odd_lam_re, odd_lam_im, odd_log_dt, odd_b_re, odd_b_im,
              odd_c_re, odd_c_im, odd_d_skip, odd_glu_wa, odd_glu_wg, odd_norm2, odd_router,
              odd_moe_w1, odd_moe_w3, odd_moe_w2, final_norm):
    h = x
    for layer in range(DEPTH):
        j = layer // 2
        if layer % 2 == 0:
            h = h + even_mixer(rmsnorm(h, even_norm1[j]), even_w_in[j], even_conv_w[j],
                               even_conv_b[j], even_cln_g[j], even_cln_b[j], even_w_out[j])
            h = h + swiglu(rmsnorm(h, even_norm2[j]), even_ffn_w1[j], even_ffn_w3[j], even_ffn_w2[j])
        else:
            h = h + ssm_mixer(rmsnorm(h, odd_norm1[j]), odd_ssm_w_in[j], odd_lam_re[j], odd_lam_im[j],
                              odd_log_dt[j], odd_b_re[j], odd_b_im[j], odd_c_re[j], odd_c_im[j],
                              odd_d_skip[j], odd_glu_wa[j], odd_glu_wg[j])
            h = h + moe_ffn(rmsnorm(h, odd_norm2[j]), odd_router[j], odd_moe_w1[j],
                            odd_moe_w3[j], odd_moe_w2[j])
    return rmsnorm(h, final_norm)
```

```python
import functools
import math

import jax
import jax.numpy as jnp
from jax import lax
from jax.experimental import pallas as pl
from jax.experimental.pallas import tpu as pltpu

F32 = jnp.float32
BF16 = jnp.bfloat16

NORM_EPS = 1e-5
N_HEADS = 8
HEAD_DIM = 128
D_ATT = N_HEADS * HEAD_DIM
DILATIONS = (1, 4, 16)
BAND = 128
ATT_CHUNK = BAND * DILATIONS[-1]
ALIBI_MAX_BIAS = 8.0
D_CONV = 1024
CONV_WIDTH = 31
CONV_HALO = 32
SSM_GROUP = 16
SSM_STATE = 64
SSM_CHUNK = 16
N_EXPERTS = 8
TOP_K = 2

VMEM_LIMIT = 56 * 1024 * 1024


def _cparams(sem):
    return pltpu.CompilerParams(dimension_semantics=sem, vmem_limit_bytes=VMEM_LIMIT)


def _rms(x, g):
    return x * lax.rsqrt(jnp.mean(x * x, axis=-1, keepdims=True) + NORM_EPS) * g


def _rmsnorm_kernel(x_ref, g_ref, o_ref):
    o_ref[...] = _rms(x_ref[...], g_ref[...]).astype(o_ref.dtype)


def rmsnorm(x, g, out_dtype, tm=512):
    s, d = x.shape
    return pl.pallas_call(
        _rmsnorm_kernel,
        grid=(s // tm,),
        in_specs=[pl.BlockSpec((tm, d), lambda m: (m, 0)),
                  pl.BlockSpec((1, d), lambda m: (0, 0))],
        out_specs=pl.BlockSpec((tm, d), lambda m: (m, 0)),
        out_shape=jax.ShapeDtypeStruct((s, d), out_dtype),
        compiler_params=_cparams(("parallel",)),
    )(x, g.reshape(1, d))


def _matmul_kernel(x_ref, w_ref, o_ref, wb_ref):
    @pl.when(pl.program_id(1) == 0)
    def _():
        wb_ref[...] = w_ref[...].astype(BF16)

    o_ref[...] = jnp.dot(x_ref[...], wb_ref[...], preferred_element_type=F32).astype(o_ref.dtype)


def matmul(x, w, out_dtype=F32, tm=512, tn=1024):
    s, k = x.shape
    n = w.shape[1]
    tn = min(tn, n)
    return pl.pallas_call(
        _matmul_kernel,
        grid=(n // tn, s // tm),
        in_specs=[pl.BlockSpec((tm, k), lambda j, m: (m, 0)),
                  pl.BlockSpec((k, tn), lambda j, m: (0, j))],
        out_specs=pl.BlockSpec((tm, tn), lambda j, m: (m, j)),
        out_shape=jax.ShapeDtypeStruct((s, n), out_dtype),
        scratch_shapes=[pltpu.VMEM((k, tn), BF16)],
        compiler_params=_cparams(("arbitrary", "arbitrary")),
    )(x, w)


def _attn_kernel(q_ref, kp_ref, kc_ref, vp_ref, vc_ref, o_ref, kk, vv, osc, lsc):
    h = pl.program_id(0)
    c = pl.program_id(1)
    kk[0:ATT_CHUNK, :] = kp_ref[...]
    kk[ATT_CHUNK:2 * ATT_CHUNK, :] = kc_ref[...]
    vv[0:ATT_CHUNK, :] = vp_ref[...]
    vv[ATT_CHUNK:2 * ATT_CHUNK, :] = vc_ref[...]

    qi = lax.broadcasted_iota(jnp.int32, (BAND, 2 * BAND), 0)
    kj = lax.broadcasted_iota(jnp.int32, (BAND, 2 * BAND), 1)
    dist = BAND + qi - kj
    band = (dist >= 0) & (dist <= BAND)
    head = jnp.full((BAND, 2 * BAND), h + 1, jnp.int32).astype(F32)
    slope = jnp.exp2(-(ALIBI_MAX_BIAS / N_HEADS) * head)
    sdist = slope * dist.astype(F32)
    scale = HEAD_DIM ** -0.5
    blocks = ATT_CHUNK // BAND

    for g, d in enumerate(DILATIONS):
        bias = -sdist * float(d)
        sub_per_chunk = ATT_CHUNK // d

        def body(b, carry, g=g, d=d, bias=bias, sub_per_chunk=sub_per_chunk):
            n = b // d
            r = b % d
            qstart = n * (BAND * d) + r
            kstart = ATT_CHUNK + (n - 1) * (BAND * d) + r
            q = q_ref[pl.ds(qstart, BAND, stride=d), :].astype(BF16)
            k = kk[pl.ds(kstart, 2 * BAND, stride=d), :].astype(BF16)
            v = vv[pl.ds(kstart, 2 * BAND, stride=d), :].astype(BF16)
            s = lax.dot_general(q, k, (((1,), (1,)), ((), ())), preferred_element_type=F32) * scale
            kpos0 = c * sub_per_chunk + (n - 1) * BAND
            valid = band & ((kj + kpos0) >= 0)
            s = jnp.where(valid, s + bias, -jnp.inf)
            m = jnp.max(s, axis=-1, keepdims=True)
            p = jnp.exp(s - m)
            den = jnp.sum(p, axis=-1, keepdims=True)
            o = jnp.dot(p.astype(BF16), v, preferred_element_type=F32) / den
            lse = m + jnp.log(den)
            osc[g, pl.ds(qstart, BAND, stride=d), :] = o
            lsc[g, pl.ds(qstart, BAND, stride=d), :] = jnp.broadcast_to(lse, (BAND, HEAD_DIM))
            return carry

        lax.fori_loop(0, blocks, body, 0)

    l0, l1, l2 = lsc[0], lsc[1], lsc[2]
    m = jnp.maximum(jnp.maximum(l0, l1), l2)
    w0, w1, w2 = jnp.exp(l0 - m), jnp.exp(l1 - m), jnp.exp(l2 - m)
    att = (w0 * osc[0] + w1 * osc[1] + w2 * osc[2]) / (w0 + w1 + w2)
    o_ref[...] = att.astype(o_ref.dtype)


def dilated_attention(proj):
    s = proj.shape[0]
    blk = (ATT_CHUNK, HEAD_DIM)
    prev = lambda c: jnp.maximum(c - 1, 0)
    return pl.pallas_call(
        _attn_kernel,
        grid=(N_HEADS, s // ATT_CHUNK),
        in_specs=[pl.BlockSpec(blk, lambda h, c: (c, h)),
                  pl.BlockSpec(blk, lambda h, c: (prev(c), N_HEADS + h)),
                  pl.BlockSpec(blk, lambda h, c: (c, N_HEADS + h)),
                  pl.BlockSpec(blk, lambda h, c: (prev(c), 2 * N_HEADS + h)),
                  pl.BlockSpec(blk, lambda h, c: (c, 2 * N_HEADS + h))],
        out_specs=pl.BlockSpec(blk, lambda h, c: (c, h)),
        out_shape=jax.ShapeDtypeStruct((s, D_ATT), BF16),
        scratch_shapes=[pltpu.VMEM((2 * ATT_CHUNK, HEAD_DIM), F32),
                        pltpu.VMEM((2 * ATT_CHUNK, HEAD_DIM), F32),
                        pltpu.VMEM((len(DILATIONS), ATT_CHUNK, HEAD_DIM), F32),
                        pltpu.VMEM((len(DILATIONS), ATT_CHUNK, HEAD_DIM), F32)],
        compiler_params=_cparams(("parallel", "parallel")),
    )(proj, proj, proj, proj, proj)


def _conv_kernel(val_ref, gate_ref, hval_ref, hgate_ref, w_ref, b_ref, lg_ref, lb_ref, o_ref, glu):
    m = pl.program_id(0)
    tm = val_ref.shape[0]
    halo = hval_ref[...] * jax.nn.sigmoid(hgate_ref[...])
    glu[0:CONV_HALO, :] = jnp.where(m > 0, halo, 0.0)
    glu[CONV_HALO:CONV_HALO + tm, :] = val_ref[...] * jax.nn.sigmoid(gate_ref[...])
    acc = jnp.broadcast_to(b_ref[...], (tm, D_CONV))
    first = CONV_HALO - (CONV_WIDTH - 1)
    for j in range(CONV_WIDTH):
        acc = acc + w_ref[j:j + 1, :] * glu[pl.ds(first + j, tm), :]
    mu = jnp.mean(acc, axis=-1, keepdims=True)
    cen = acc - mu
    var = jnp.mean(cen * cen, axis=-1, keepdims=True)
    y = cen * lax.rsqrt(var + NORM_EPS) * lg_ref[...] + lb_ref[...]
    o_ref[...] = (y * jax.nn.sigmoid(y)).astype(o_ref.dtype)


def conv_module(proj, conv_w, conv_b, ln_g, ln_b, tm=512):
    s = proj.shape[0]
    vcol = 3 * D_ATT // D_CONV
    gcol = vcol + 1
    per = tm // CONV_HALO
    hrow = lambda m: jnp.maximum(m * per - 1, 0)
    vec = pl.BlockSpec((1, D_CONV), lambda m: (0, 0))
    return pl.pallas_call(
        _conv_kernel,
        grid=(s // tm,),
        in_specs=[pl.BlockSpec((tm, D_CONV), lambda m: (m, vcol)),
                  pl.BlockSpec((tm, D_CONV), lambda m: (m, gcol)),
                  pl.BlockSpec((CONV_HALO, D_CONV), lambda m: (hrow(m), vcol)),
                  pl.BlockSpec((CONV_HALO, D_CONV), lambda m: (hrow(m), gcol)),
                  pl.BlockSpec((CONV_WIDTH, D_CONV), lambda m: (0, 0)),
                  vec, vec, vec],
        out_specs=pl.BlockSpec((tm, D_CONV), lambda m: (m, 0)),
        out_shape=jax.ShapeDtypeStruct((s, D_CONV), BF16),
        scratch_shapes=[pltpu.VMEM((CONV_HALO + tm, D_CONV), F32)],
        compiler_params=_cparams(("parallel",)),
    )(proj, proj, proj, proj, conv_w, conv_b.reshape(1, -1), ln_g.reshape(1, -1), ln_b.reshape(1, -1))


def _outproj_kernel(att_ref, cv_ref, w_ref, h_ref, g_ref, hnew_ref, hn_ref, wb_ref):
    @pl.when(pl.program_id(0) == 0)
    def _():
        wb_ref[...] = w_ref[...].astype(BF16)

    y = jnp.dot(att_ref[...], wb_ref[0:D_ATT, :], preferred_element_type=F32)
    y = y + jnp.dot(cv_ref[...], wb_ref[D_ATT:D_ATT + D_CONV, :], preferred_element_type=F32)
    hnew = h_ref[...] + y
    hnew_ref[...] = hnew
    hn_ref[...] = _rms(hnew, g_ref[...]).astype(hn_ref.dtype)


def outproj_residual_norm(att, cv, w_out, h, g, tm=256):
    s, d = h.shape
    kin = w_out.shape[0]
    row = lambda m: (m, 0)
    fixed = lambda m: (0, 0)
    return pl.pallas_call(
        _outproj_kernel,
        grid=(s // tm,),
        in_specs=[pl.BlockSpec((tm, D_ATT), row),
                  pl.BlockSpec((tm, D_CONV), row),
                  pl.BlockSpec((kin, d), fixed),
                  pl.BlockSpec((tm, d), row),
                  pl.BlockSpec((1, d), fixed)],
        out_specs=[pl.BlockSpec((tm, d), row), pl.BlockSpec((tm, d), row)],
        out_shape=[jax.ShapeDtypeStruct((s, d), F32), jax.ShapeDtypeStruct((s, d), BF16)],
        scratch_shapes=[pltpu.VMEM((kin, d), BF16)],
        compiler_params=_cparams(("arbitrary",)),
    )(att, cv, w_out, h, g.reshape(1, d))


def _gluproj_kernel(y_ref, wa_ref, wg_ref, h_ref, g_ref, hnew_ref, hn_ref, wab_ref, wgb_ref):
    @pl.when(pl.program_id(0) == 0)
    def _():
        wab_ref[...] = wa_ref[...].astype(BF16)
        wgb_ref[...] = wg_ref[...].astype(BF16)

    y = y_ref[...]
    a = jnp.dot(y, wab_ref[...], preferred_element_type=F32)
    gt = jnp.dot(y, wgb_ref[...], preferred_element_type=F32)
    hnew = h_ref[...] + a * jax.nn.sigmoid(gt)
    hnew_ref[...] = hnew
    hn_ref[...] = _rms(hnew, g_ref[...]).astype(hn_ref.dtype)


def gluproj_residual_norm(y, wa, wg, h, g, tm=256):
    s, d = h.shape
    kin = wa.shape[0]
    row = lambda m: (m, 0)
    fixed = lambda m: (0, 0)
    return pl.pallas_call(
        _gluproj_kernel,
        grid=(s // tm,),
        in_specs=[pl.BlockSpec((tm, kin), row),
                  pl.BlockSpec((kin, d), fixed),
                  pl.BlockSpec((kin, d), fixed),
                  pl.BlockSpec((tm, d), row),
                  pl.BlockSpec((1, d), fixed)],
        out_specs=[pl.BlockSpec((tm, d), row), pl.BlockSpec((tm, d), row)],
        out_shape=[jax.ShapeDtypeStruct((s, d), F32), jax.ShapeDtypeStruct((s, d), F32)],
        scratch_shapes=[pltpu.VMEM((kin, d), BF16), pltpu.VMEM((kin, d), BF16)],
        compiler_params=_cparams(("arbitrary",)),
    )(y, wa, wg, h, g.reshape(1, d))


def _swiglu_step(xb, w1_ref, w3_ref, w2_ref):
    a = jnp.dot(xb, w1_ref[...].astype(BF16), preferred_element_type=F32)
    b = jnp.dot(xb, w3_ref[...].astype(BF16), preferred_element_type=F32)
    p = (a * jax.nn.sigmoid(a) * b).astype(BF16)
    return jnp.dot(p, w2_ref[...].astype(BF16), preferred_element_type=F32)


def _ffn_kernel(x_ref, w1_ref, w3_ref, w2_ref, h_ref, g_ref, hnew_ref, hn_ref, acc_ref):
    f = pl.program_id(1)

    @pl.when(f == 0)
    def _():
        acc_ref[...] = jnp.zeros_like(acc_ref)

    acc_ref[...] += _swiglu_step(x_ref[...], w1_ref, w3_ref, w2_ref)

    @pl.when(f == pl.num_programs(1) - 1)
    def _():
        hnew = h_ref[...] + acc_ref[...]
        hnew_ref[...] = hnew
        hn_ref[...] = _rms(hnew, g_ref[...]).astype(hn_ref.dtype)


def ffn_residual_norm(x, w1, w3, w2, h, g, tm=512, tf=256):
    s, d = h.shape
    dff = w1.shape[1]
    row = lambda m, f: (m, 0)
    return pl.pallas_call(
        _ffn_kernel,
        grid=(s // tm, dff // tf),
        in_specs=[pl.BlockSpec((tm, d), row),
                  pl.BlockSpec((d, tf), lambda m, f: (0, f)),
                  pl.BlockSpec((d, tf), lambda m, f: (0, f)),
                  pl.BlockSpec((tf, d), lambda m, f: (f, 0)),
                  pl.BlockSpec((tm, d), row),
                  pl.BlockSpec((1, d), lambda m, f: (0, 0))],
        out_specs=[pl.BlockSpec((tm, d), row), pl.BlockSpec((tm, d), row)],
        out_shape=[jax.ShapeDtypeStruct((s, d), F32), jax.ShapeDtypeStruct((s, d), BF16)],
        scratch_shapes=[pltpu.VMEM((tm, d), F32)],
        compiler_params=_cparams(("parallel", "arbitrary")),
    )(x, w1, w3, w2, h, g.reshape(1, d))


def _ssm_tables(lam_re, lam_im, log_dt, b_re, b_im, c_re, c_im, d_skip, n_doublings):
    L = SSM_CHUNK
    G, P = lam_re.shape
    dt = jnp.exp(log_dt)[:, None]
    mag = jnp.exp(lam_re * dt)
    a_re = mag * jnp.cos(lam_im * dt)
    a_im = mag * jnp.sin(lam_im * dt)
    inv = 1.0 / (lam_re * lam_re + lam_im * lam_im)
    f_re = ((a_re - 1.0) * lam_re + a_im * lam_im) * inv
    f_im = (a_im * lam_re - (a_re - 1.0) * lam_im) * inv
    bb_re = f_re[..., None] * b_re - f_im[..., None] * b_im
    bb_im = f_re[..., None] * b_im + f_im[..., None] * b_re

    pw_re = [jnp.ones_like(a_re)]
    pw_im = [jnp.zeros_like(a_im)]
    for _ in range(L):
        pr, pi = pw_re[-1], pw_im[-1]
        pw_re.append(pr * a_re - pi * a_im)
        pw_im.append(pr * a_im + pi * a_re)
    pw_re = jnp.stack(pw_re)
    pw_im = jnp.stack(pw_im)

    ab_re = pw_re[:L, :, :, None] * bb_re[None] - pw_im[:L, :, :, None] * bb_im[None]
    ab_im = pw_re[:L, :, :, None] * bb_im[None] + pw_im[:L, :, :, None] * bb_re[None]
    hi = lax.Precision.HIGHEST
    kern = (jnp.einsum('gcp,tgpd->tgcd', c_re, ab_re, precision=hi)
            - jnp.einsum('gcp,tgpd->tgcd', c_im, ab_im, precision=hi))
    C = SSM_GROUP
    kern = kern.at[0].add(jnp.eye(C, dtype=F32)[None] * d_skip.reshape(G, C)[:, :, None])
    j_idx = jnp.arange(L)[:, None]
    s_idx = jnp.arange(L)[None, :]
    lag = s_idx - j_idx
    blocks = kern[jnp.clip(lag, 0, L - 1)]
    blocks = jnp.where((lag >= 0)[:, :, None, None, None], blocks, 0.0)
    toep = blocks.transpose(2, 0, 4, 1, 3).reshape(G, L * C, L * C)

    rev_re = ab_re[::-1]
    rev_im = ab_im[::-1]
    bend = jnp.concatenate([rev_re.transpose(1, 0, 3, 2).reshape(G, L * C, P),
                            rev_im.transpose(1, 0, 3, 2).reshape(G, L * C, P)], axis=-1)

    ca_re = c_re[None] * pw_re[1:, :, None, :] - c_im[None] * pw_im[1:, :, None, :]
    ca_im = c_re[None] * pw_im[1:, :, None, :] + c_im[None] * pw_re[1:, :, None, :]
    cdec = jnp.concatenate([ca_re.transpose(1, 3, 0, 2).reshape(G, P, L * C),
                            (-ca_im).transpose(1, 3, 0, 2).reshape(G, P, L * C)], axis=1)

    sq_re, sq_im = pw_re[L], pw_im[L]
    same, cross = [], []
    for _ in range(n_doublings):
        same.append(jnp.concatenate([sq_re, sq_re], axis=-1))
        cross.append(jnp.concatenate([-sq_im, sq_im], axis=-1))
        sq_re, sq_im = sq_re * sq_re - sq_im * sq_im, 2.0 * sq_re * sq_im
    return toep, bend, cdec, jnp.stack(same, axis=1), jnp.stack(cross, axis=1)


def _ssm_kernel(u_ref, toep_ref, bend_ref, cdec_ref, same_ref, cross_ref, y_ref):
    u = u_ref[...].astype(BF16)
    nc = u.shape[0]
    x = jnp.dot(u, bend_ref[...].astype(BF16), preferred_element_type=F32)
    row = lax.broadcasted_iota(jnp.int32, x.shape, 0)
    for k in range(same_ref.shape[0]):
        shift = 1 << k
        sh = jnp.where(row >= shift, pltpu.roll(x, shift, 0), 0.0)
        x = x + same_ref[k:k + 1, :] * sh + cross_ref[k:k + 1, :] * pltpu.roll(sh, SSM_STATE, 1)
    prev = jnp.where(row >= 1, pltpu.roll(x, 1, 0), 0.0).astype(BF16)
    y = jnp.dot(u, toep_ref[...].astype(BF16), preferred_element_type=F32)
    y = y + jnp.dot(prev, cdec_ref[...].astype(BF16), preferred_element_type=F32)
    y_ref[...] = jax.nn.gelu(y).astype(y_ref.dtype)


def ssm_core(u, params):
    s = u.shape[0]
    L, C, P = SSM_CHUNK, SSM_GROUP, SSM_STATE
    nc = s // L
    n_doublings = max(1, (nc - 1).bit_length())
    toep, bend, cdec, same, cross = _ssm_tables(*params, n_doublings=n_doublings)
    G, LC, _ = toep.shape
    ug = u.reshape(nc, L, G, C).transpose(2, 0, 1, 3).reshape(G, nc, LC)
    grp = lambda g: (g, 0, 0)
    y = pl.pallas_call(
        _ssm_kernel,
        grid=(G,),
        in_specs=[pl.BlockSpec((None, nc, LC), grp),
                  pl.BlockSpec((None, LC, LC), grp),
                  pl.BlockSpec((None, LC, 2 * P), grp),
                  pl.BlockSpec((None, 2 * P, LC), grp),
                  pl.BlockSpec((None, n_doublings, 2 * P), grp),
                  pl.BlockSpec((None, n_doublings, 2 * P), grp)],
        out_specs=pl.BlockSpec((None, nc, LC), grp),
        out_shape=jax.ShapeDtypeStruct((G, nc, LC), F32),
        compiler_params=_cparams(("parallel",)),
    )(ug, toep, bend, cdec, same, cross)
    y = y.reshape(G, nc, L, C).transpose(1, 2, 0, 3).reshape(s, G * C)
    return y.astype(BF16)


def _router_kernel(x_ref, r_ref, o_ref):
    logits = jnp.dot(x_ref[...], r_ref[...], preferred_element_type=F32, precision=lax.Precision.HIGHEST)
    lane = lax.broadcasted_iota(jnp.int32, logits.shape, 1)
    logits = jnp.where(lane < N_EXPERTS, logits, -jnp.inf)
    big = jnp.int32(logits.shape[1])
    v1 = jnp.max(logits, axis=-1, keepdims=True)
    i1 = jnp.min(jnp.where(logits == v1, lane, big), axis=-1, keepdims=True)
    rest = jnp.where(lane == i1, -jnp.inf, logits)
    v2 = jnp.max(rest, axis=-1, keepdims=True)
    i2 = jnp.min(jnp.where(rest == v2, lane, big), axis=-1, keepdims=True)
    e2 = jnp.exp(v2 - v1)
    g1 = 1.0 / (1.0 + e2)
    g2 = e2 / (1.0 + e2)
    out = jnp.where(lane == 0, i1.astype(F32), 0.0)
    out = jnp.where(lane == 1, i2.astype(F32), out)
    out = jnp.where(lane == 2, g1, out)
    out = jnp.where(lane == 3, g2, out)
    o_ref[...] = out


def router_top2(x, router, tm=512):
    s, d = x.shape
    lanes = 128
    rp = jnp.zeros((d, lanes), F32).at[:, :N_EXPERTS].set(router)
    return pl.pallas_call(
        _router_kernel,
        grid=(s // tm,),
        in_specs=[pl.BlockSpec((tm, d), lambda m: (m, 0)),
                  pl.BlockSpec((d, lanes), lambda m: (0, 0))],
        out_specs=pl.BlockSpec((tm, lanes), lambda m: (m, 0)),
        out_shape=jax.ShapeDtypeStruct((s, lanes), F32),
        compiler_params=_cparams(("parallel",)),
    )(x, rp)


def _moe_kernel(texp_ref, trows_ref, src_ref, dst_ref,
                x_hbm, w1_ref, w3_ref, w2_ref, y_hbm, xbuf, xb, acc, gsem, ssem):
    t = pl.program_id(0)
    f = pl.program_id(1)
    tm = xbuf.shape[0]
    rows = trows_ref[t]
    base = t * tm

    def gather_copy(i):
        return pltpu.make_async_copy(x_hbm.at[pl.ds(src_ref[base + i], 1), :], xbuf.at[pl.ds(i, 1), :], gsem)

    def scatter_copy(i):
        return pltpu.make_async_copy(acc.at[pl.ds(i, 1), :], y_hbm.at[pl.ds(dst_ref[base + i], 1), :], ssem)

    def start_all(make):
        def body(i, carry):
            make(i).start()
            return carry
        lax.fori_loop(0, rows, body, 0)

    def wait_all(make):
        def body(i, carry):
            make(i).wait()
            return carry
        lax.fori_loop(0, rows, body, 0)

    @pl.when((t == 0) & (f == 0))
    def _():
        xbuf[...] = jnp.zeros_like(xbuf)

    @pl.when(rows > 0)
    def _():
        @pl.when(f == 0)
        def _():
            start_all(gather_copy)
            wait_all(gather_copy)
            xb[...] = xbuf[...].astype(BF16)
            acc[...] = jnp.zeros_like(acc)

        acc[...] += _swiglu_step(xb[...], w1_ref, w3_ref, w2_ref)

        @pl.when(f == pl.num_programs(1) - 1)
        def _():
            start_all(scatter_copy)
            wait_all(scatter_copy)


def moe_experts(x, w1, w3, w2, tile_expert, tile_rows, src_tok, dst_row, tm, tf=512):
    s, d = x.shape
    dff = w1.shape[2]
    n_tiles = tile_expert.shape[0]
    nf = dff // tf
    last = nf - 1

    def fsel(t, f, trows):
        return jnp.where(trows[t] > 0, f, last)

    grid_spec = pltpu.PrefetchScalarGridSpec(
        num_scalar_prefetch=4,
        grid=(n_tiles, nf),
        in_specs=[pl.BlockSpec(memory_space=pl.ANY),
                  pl.BlockSpec((None, d, tf), lambda t, f, te, tr, sr, ds: (te[t], 0, fsel(t, f, tr))),
                  pl.BlockSpec((None, d, tf), lambda t, f, te, tr, sr, ds: (te[t], 0, fsel(t, f, tr))),
                  pl.BlockSpec((None, tf, d), lambda t, f, te, tr, sr, ds: (te[t], fsel(t, f, tr), 0))],
        out_specs=pl.BlockSpec(memory_space=pl.ANY),
        scratch_shapes=[pltpu.VMEM((tm, d), F32),
                        pltpu.VMEM((tm, d), BF16),
                        pltpu.VMEM((tm, d), F32),
                        pltpu.SemaphoreType.DMA(()),
                        pltpu.SemaphoreType.DMA(())],
    )
    return pl.pallas_call(
        _moe_kernel,
        grid_spec=grid_spec,
        out_shape=jax.ShapeDtypeStruct((TOP_K * s, d), F32),
        compiler_params=_cparams(("arbitrary", "arbitrary")),
    )(tile_expert, tile_rows, src_tok, dst_row, x, w1, w3, w2)


def _route_plan(e1, e2, tm, n_tiles):
    s = e1.shape[0]
    experts = jnp.concatenate([e1, e2])
    onehot = (experts[:, None] == jnp.arange(N_EXPERTS, dtype=jnp.int32)[None, :]).astype(jnp.int32)
    csum = jnp.cumsum(onehot, axis=0)
    rank = jnp.sum((csum - onehot) * onehot, axis=1)
    counts = csum[-1]
    tiles_per = (counts + tm - 1) // tm
    tile_end = jnp.cumsum(tiles_per)
    tile_start = tile_end - tiles_per
    pos = tile_start[experts] * tm + rank
    n_rows = n_tiles * tm
    src_tok = jnp.zeros((n_rows,), jnp.int32).at[pos].set(jnp.arange(TOP_K * s, dtype=jnp.int32) % s)
    dst_row = jnp.zeros((n_rows,), jnp.int32).at[pos].set(jnp.arange(TOP_K * s, dtype=jnp.int32))
    tid = jnp.arange(n_tiles, dtype=jnp.int32)
    texp = jnp.minimum(jnp.sum((tid[:, None] >= tile_end[None, :]).astype(jnp.int32), axis=1), N_EXPERTS - 1)
    used = tid < tile_end[-1]
    local = tid - tile_start[texp]
    trows = jnp.where(used, jnp.clip(counts[texp] - local * tm, 0, tm), 0).astype(jnp.int32)
    last_used = texp[jnp.maximum(tile_end[-1] - 1, 0)]
    texp = jnp.where(used, texp, last_used).astype(jnp.int32)
    return texp, trows, src_tok, dst_row


def _combine_kernel(h_ref, y1_ref, y2_ref, gt_ref, g_ref, *out_refs, want_h):
    gates = gt_ref[...]
    hnew = h_ref[...] + gates[:, 2:3] * y1_ref[...] + gates[:, 3:4] * y2_ref[...]
    if want_h:
        out_refs[0][...] = hnew
    out_refs[-1][...] = _rms(hnew, g_ref[...]).astype(out_refs[-1].dtype)


def combine_residual_norm(h, y, route, g, want_h, norm_dtype, tm=512):
    s, d = h.shape
    nb = s // tm
    row = lambda m: (m, 0)
    outs_spec = [pl.BlockSpec((tm, d), row)]
    outs_shape = [jax.ShapeDtypeStruct((s, d), norm_dtype)]
    if want_h:
        outs_spec = [pl.BlockSpec((tm, d), row)] + outs_spec
        outs_shape = [jax.ShapeDtypeStruct((s, d), F32)] + outs_shape
    return pl.pallas_call(
        functools.partial(_combine_kernel, want_h=want_h),
        grid=(nb,),
        in_specs=[pl.BlockSpec((tm, d), row),
                  pl.BlockSpec((tm, d), row),
                  pl.BlockSpec((tm, d), lambda m: (m + nb, 0)),
                  pl.BlockSpec((tm, route.shape[1]), row),
                  pl.BlockSpec((1, d), lambda m: (0, 0))],
        out_specs=outs_spec,
        out_shape=outs_shape,
        compiler_params=_cparams(("parallel",)),
    )(h, y, y, route, g.reshape(1, d))


MOE_TM = 512


def kernel(x, even_norm1, even_w_in, even_conv_w, even_conv_b, even_cln_g, even_cln_b, even_w_out, even_norm2, even_ffn_w1, even_ffn_w3, even_ffn_w2, odd_norm1, odd_ssm_w_in, odd_lam_re, odd_lam_im, odd_log_dt, odd_b_re, odd_b_im, odd_c_re, odd_c_im, odd_d_skip, odd_glu_wa, odd_glu_wg, odd_norm2, odd_router, odd_moe_w1, odd_moe_w3, odd_moe_w2, final_norm):
    b, s, d = x.shape
    assert b == 1
    depth = even_norm1.shape[0] + odd_norm1.shape[0]
    h = x.reshape(s, d)
    hn = rmsnorm(h, even_norm1[0], BF16)
    n_tiles = (TOP_K * s) // MOE_TM + N_EXPERTS
    out = None
    for layer in range(depth):
        j = layer // 2
        if layer % 2 == 0:
            proj = matmul(hn, even_w_in[j])
            att = dilated_attention(proj)
            cv = conv_module(proj, even_conv_w[j], even_conv_b[j], even_cln_g[j], even_cln_b[j])
            h, hn = outproj_residual_norm(att, cv, even_w_out[j], h, even_norm2[j])
            h, hn = ffn_residual_norm(hn, even_ffn_w1[j], even_ffn_w3[j], even_ffn_w2[j], h, odd_norm1[j])
        else:
            u = matmul(hn, odd_ssm_w_in[j])
            y = ssm_core(u, (odd_lam_re[j], odd_lam_im[j], odd_log_dt[j], odd_b_re[j], odd_b_im[j],
                             odd_c_re[j], odd_c_im[j], odd_d_skip[j]))
            h, hn = gluproj_residual_norm(y, odd_glu_wa[j], odd_glu_wg[j], h, odd_norm2[j])
            route = router_top2(hn, odd_router[j])
            e1 = route[:, 0].astype(jnp.int32)
            e2 = route[:, 1].astype(jnp.int32)
            texp, trows, src_tok, dst_row = _route_plan(e1, e2, MOE_TM, n_tiles)
            ys = moe_experts(hn, odd_moe_w1[j], odd_moe_w3[j], odd_moe_w2[j],
                             texp, trows, src_tok, dst_row, MOE_TM)
            if layer + 1 < depth:
                h, hn = combine_residual_norm(h, ys, route, even_norm1[j + 1], True, BF16)
            else:
                (out,) = combine_residual_norm(h, ys, route, final_norm, False, F32)
    return out.reshape(b, s, d)
```

```python
import functools
import math

import jax
import jax.numpy as jnp
from jax import lax
from jax.experimental import pallas as pl
from jax.experimental.pallas import tpu as pltpu

F32 = jnp.float32
BF16 = jnp.bfloat16

NORM_EPS = 1e-5
N_HEADS = 8
HEAD_DIM = 128
D_ATT = N_HEADS * HEAD_DIM
DILATIONS = (1, 4, 16)
BAND = 128
ATT_CHUNK = BAND * DILATIONS[-1]
ATT_UNROLL = 4
ALIBI_MAX_BIAS = 8.0
D_CONV = 1024
CONV_WIDTH = 31
CONV_HALO = 32
SSM_GROUP = 16
SSM_STATE = 64
SSM_CHUNK = 16
N_EXPERTS = 8
TOP_K = 2
DMA_UNROLL = 8

VMEM_LIMIT = 56 * 1024 * 1024


def _cparams(sem):
    return pltpu.CompilerParams(dimension_semantics=sem, vmem_limit_bytes=VMEM_LIMIT)


def _rms(x, g):
    return x * lax.rsqrt(jnp.mean(x * x, axis=-1, keepdims=True) + NORM_EPS) * g


def _rmsnorm_kernel(x_ref, g_ref, o_ref):
    o_ref[...] = _rms(x_ref[...], g_ref[...]).astype(o_ref.dtype)


def rmsnorm(x, g, out_dtype, tm=512):
    s, d = x.shape
    return pl.pallas_call(
        _rmsnorm_kernel,
        grid=(s // tm,),
        in_specs=[pl.BlockSpec((tm, d), lambda m: (m, 0)),
                  pl.BlockSpec((1, d), lambda m: (0, 0))],
        out_specs=pl.BlockSpec((tm, d), lambda m: (m, 0)),
        out_shape=jax.ShapeDtypeStruct((s, d), out_dtype),
        compiler_params=_cparams(("parallel",)),
    )(x, g.reshape(1, d))


def _matmul_kernel(x_ref, w_ref, o_ref, wb_ref):
    @pl.when(pl.program_id(1) == 0)
    def _():
        wb_ref[...] = w_ref[...].astype(BF16)

    o_ref[...] = jnp.dot(x_ref[...], wb_ref[...], preferred_element_type=F32).astype(o_ref.dtype)


def matmul(x, w, layer, out_dtype=F32, tm=512, tn=1024):
    s, k = x.shape
    n = w.shape[2]
    tn = min(tn, n)
    return pl.pallas_call(
        _matmul_kernel,
        grid=(n // tn, s // tm),
        in_specs=[pl.BlockSpec((tm, k), lambda j, m: (m, 0)),
                  pl.BlockSpec((None, k, tn), lambda j, m: (layer, 0, j))],
        out_specs=pl.BlockSpec((tm, tn), lambda j, m: (m, j)),
        out_shape=jax.ShapeDtypeStruct((s, n), out_dtype),
        scratch_shapes=[pltpu.VMEM((k, tn), BF16)],
        compiler_params=_cparams(("arbitrary", "arbitrary")),
    )(x, w)


def _resnorm_matmul_kernel(h_ref, y_ref, g_ref, w_ref, hnew_ref, o_ref, wb_ref):
    @pl.when(pl.program_id(0) == 0)
    def _():
        wb_ref[...] = w_ref[...].astype(BF16)

    hnew = h_ref[...] + y_ref[...]
    hnew_ref[...] = hnew
    xn = _rms(hnew, g_ref[...]).astype(BF16)
    o_ref[...] = jnp.dot(xn, wb_ref[...], preferred_element_type=F32).astype(o_ref.dtype)


def residual_norm_matmul(h, y, g, w, layer, tm=256):
    s, d = h.shape
    n = w.shape[2]
    row = lambda m: (m, 0)
    return pl.pallas_call(
        _resnorm_matmul_kernel,
        grid=(s // tm,),
        in_specs=[pl.BlockSpec((tm, d), row),
                  pl.BlockSpec((tm, d), row),
                  pl.BlockSpec((1, d), lambda m: (0, 0)),
                  pl.BlockSpec((None, d, n), lambda m: (layer, 0, 0))],
        out_specs=[pl.BlockSpec((tm, d), row), pl.BlockSpec((tm, n), row)],
        out_shape=[jax.ShapeDtypeStruct((s, d), F32), jax.ShapeDtypeStruct((s, n), F32)],
        scratch_shapes=[pltpu.VMEM((d, n), BF16)],
        compiler_params=_cparams(("arbitrary",)),
    )(h, y, g.reshape(1, d), w)


def _attn_kernel(q_ref, kp_ref, kc_ref, vp_ref, vc_ref, o_ref, kk, vv, osc, lsc):
    h = pl.program_id(0)
    c = pl.program_id(1)
    kk[0:ATT_CHUNK, :] = kp_ref[...]
    kk[ATT_CHUNK:2 * ATT_CHUNK, :] = kc_ref[...]
    vv[0:ATT_CHUNK, :] = vp_ref[...]
    vv[ATT_CHUNK:2 * ATT_CHUNK, :] = vc_ref[...]

    qi = lax.broadcasted_iota(jnp.int32, (BAND, 2 * BAND), 0)
    kj = lax.broadcasted_iota(jnp.int32, (BAND, 2 * BAND), 1)
    dist = BAND + qi - kj
    band = (dist >= 0) & (dist <= BAND)
    head = jnp.full((BAND, 2 * BAND), h + 1, jnp.int32).astype(F32)
    slope = jnp.exp2(-(ALIBI_MAX_BIAS / N_HEADS) * head)
    sdist = slope * dist.astype(F32)
    scale = HEAD_DIM ** -0.5
    blocks = ATT_CHUNK // BAND

    for g, d in enumerate(DILATIONS):
        bias = jnp.where(band, -sdist * float(d), -jnp.inf)
        bias_first = jnp.where(kj >= BAND, bias, -jnp.inf)

        def one_block(b, g=g, d=d, bias=bias, bias_first=bias_first):
            n = b // d
            r = b % d
            qstart = n * (BAND * d) + r
            kstart = ATT_CHUNK + (n - 1) * (BAND * d) + r
            q = q_ref[pl.ds(qstart, BAND, stride=d), :].astype(BF16)
            k = kk[pl.ds(kstart, 2 * BAND, stride=d), :].astype(BF16)
            v = vv[pl.ds(kstart, 2 * BAND, stride=d), :].astype(BF16)
            s = lax.dot_general(q, k, (((1,), (1,)), ((), ())), preferred_element_type=F32) * scale
            s = s + jnp.where((c == 0) & (n == 0), bias_first, bias)
            m = jnp.max(s, axis=-1, keepdims=True)
            p = jnp.exp(s - m)
            den = jnp.sum(p, axis=-1, keepdims=True)
            o = jnp.dot(p.astype(BF16), v, preferred_element_type=F32) / den
            lse = m + jnp.log(den)
            osc[g, pl.ds(qstart, BAND, stride=d), :] = o
            lsc[g, pl.ds(qstart, BAND, stride=d), :] = jnp.broadcast_to(lse, (BAND, HEAD_DIM))

        def body(i, carry, one_block=one_block):
            for u in range(ATT_UNROLL):
                one_block(i * ATT_UNROLL + u)
            return carry

        lax.fori_loop(0, blocks // ATT_UNROLL, body, 0)

    l0, l1, l2 = lsc[0], lsc[1], lsc[2]
    m = jnp.maximum(jnp.maximum(l0, l1), l2)
    w0, w1, w2 = jnp.exp(l0 - m), jnp.exp(l1 - m), jnp.exp(l2 - m)
    att = (w0 * osc[0] + w1 * osc[1] + w2 * osc[2]) / (w0 + w1 + w2)
    o_ref[...] = att.astype(o_ref.dtype)


def dilated_attention(proj):
    s = proj.shape[0]
    blk = (ATT_CHUNK, HEAD_DIM)
    prev = lambda c: jnp.maximum(c - 1, 0)
    return pl.pallas_call(
        _attn_kernel,
        grid=(N_HEADS, s // ATT_CHUNK),
        in_specs=[pl.BlockSpec(blk, lambda h, c: (c, h)),
                  pl.BlockSpec(blk, lambda h, c: (prev(c), N_HEADS + h)),
                  pl.BlockSpec(blk, lambda h, c: (c, N_HEADS + h)),
                  pl.BlockSpec(blk, lambda h, c: (prev(c), 2 * N_HEADS + h)),
                  pl.BlockSpec(blk, lambda h, c: (c, 2 * N_HEADS + h))],
        out_specs=pl.BlockSpec(blk, lambda h, c: (c, h)),
        out_shape=jax.ShapeDtypeStruct((s, D_ATT), BF16),
        scratch_shapes=[pltpu.VMEM((2 * ATT_CHUNK, HEAD_DIM), F32),
                        pltpu.VMEM((2 * ATT_CHUNK, HEAD_DIM), F32),
                        pltpu.VMEM((len(DILATIONS), ATT_CHUNK, HEAD_DIM), F32),
                        pltpu.VMEM((len(DILATIONS), ATT_CHUNK, HEAD_DIM), F32)],
        compiler_params=_cparams(("parallel", "parallel")),
    )(proj, proj, proj, proj, proj)


def _conv_kernel(val_ref, gate_ref, hval_ref, hgate_ref, w_ref, b_ref, lg_ref, lb_ref, o_ref, glu):
    m = pl.program_id(0)
    tm = val_ref.shape[0]
    halo = hval_ref[...] * jax.nn.sigmoid(hgate_ref[...])
    glu[0:CONV_HALO, :] = jnp.where(m > 0, halo, 0.0)
    glu[CONV_HALO:CONV_HALO + tm, :] = val_ref[...] * jax.nn.sigmoid(gate_ref[...])
    first = CONV_HALO - (CONV_WIDTH - 1)

    acc = jnp.broadcast_to(b_ref[...], (tm, D_CONV))
    for j in range(CONV_WIDTH):
        acc = acc + w_ref[j:j + 1, :] * glu[pl.ds(first + j, tm), :]
    mu = jnp.mean(acc, axis=-1, keepdims=True)
    cen = acc - mu
    var = jnp.mean(cen * cen, axis=-1, keepdims=True)
    y = cen * lax.rsqrt(var + NORM_EPS) * lg_ref[...] + lb_ref[...]
    o_ref[...] = (y * jax.nn.sigmoid(y)).astype(o_ref.dtype)


def conv_module(proj, conv_w, conv_b, ln_g, ln_b, tm=512):
    s = proj.shape[0]
    vcol = 3 * D_ATT // D_CONV
    gcol = vcol + 1
    per = tm // CONV_HALO
    hrow = lambda m: jnp.maximum(m * per - 1, 0)
    vec = pl.BlockSpec((1, D_CONV), lambda m: (0, 0))
    return pl.pallas_call(
        _conv_kernel,
        grid=(s // tm,),
        in_specs=[pl.BlockSpec((tm, D_CONV), lambda m: (m, vcol)),
                  pl.BlockSpec((tm, D_CONV), lambda m: (m, gcol)),
                  pl.BlockSpec((CONV_HALO, D_CONV), lambda m: (hrow(m), vcol)),
                  pl.BlockSpec((CONV_HALO, D_CONV), lambda m: (hrow(m), gcol)),
                  pl.BlockSpec((CONV_WIDTH, D_CONV), lambda m: (0, 0)),
                  vec, vec, vec],
        out_specs=pl.BlockSpec((tm, D_CONV), lambda m: (m, 0)),
        out_shape=jax.ShapeDtypeStruct((s, D_CONV), BF16),
        scratch_shapes=[pltpu.VMEM((CONV_HALO + tm, D_CONV), F32)],
        compiler_params=_cparams(("parallel",)),
    )(proj, proj, proj, proj, conv_w, conv_b.reshape(1, -1), ln_g.reshape(1, -1), ln_b.reshape(1, -1))


def _outproj_kernel(att_ref, cv_ref, w_ref, h_ref, g_ref, hnew_ref, hn_ref, wb_ref):
    @pl.when(pl.program_id(0) == 0)
    def _():
        wb_ref[...] = w_ref[...].astype(BF16)

    y = jnp.dot(att_ref[...], wb_ref[0:D_ATT, :], preferred_element_type=F32)
    y = y + jnp.dot(cv_ref[...], wb_ref[D_ATT:D_ATT + D_CONV, :], preferred_element_type=F32)
    hnew = h_ref[...] + y
    hnew_ref[...] = hnew
    hn_ref[...] = _rms(hnew, g_ref[...]).astype(hn_ref.dtype)


def outproj_residual_norm(att, cv, w_out, layer, h, g, tm=512):
    s, d = h.shape
    kin = w_out.shape[1]
    row = lambda m: (m, 0)
    fixed = lambda m: (0, 0)
    return pl.pallas_call(
        _outproj_kernel,
        grid=(s // tm,),
        in_specs=[pl.BlockSpec((tm, D_ATT), row),
                  pl.BlockSpec((tm, D_CONV), row),
                  pl.BlockSpec((None, kin, d), lambda m: (layer, 0, 0), pipeline_mode=pl.Buffered(1)),
                  pl.BlockSpec((tm, d), row),
                  pl.BlockSpec((1, d), fixed)],
        out_specs=[pl.BlockSpec((tm, d), row), pl.BlockSpec((tm, d), row)],
        out_shape=[jax.ShapeDtypeStruct((s, d), F32), jax.ShapeDtypeStruct((s, d), BF16)],
        scratch_shapes=[pltpu.VMEM((kin, d), BF16)],
        compiler_params=_cparams(("arbitrary",)),
    )(att, cv, w_out, h, g.reshape(1, d))


def _gluproj_kernel(y_ref, wa_ref, wg_ref, h_ref, g_ref, hnew_ref, hn_ref, wab_ref, wgb_ref):
    @pl.when(pl.program_id(0) == 0)
    def _():
        wab_ref[...] = wa_ref[...].astype(BF16)
        wgb_ref[...] = wg_ref[...].astype(BF16)

    y = y_ref[...]
    a = jnp.dot(y, wab_ref[...], preferred_element_type=F32)
    gt = jnp.dot(y, wgb_ref[...], preferred_element_type=F32)
    hnew = h_ref[...] + a * jax.nn.sigmoid(gt)
    hnew_ref[...] = hnew
    hn_ref[...] = _rms(hnew, g_ref[...]).astype(hn_ref.dtype)


def gluproj_residual_norm(y, wa, wg, layer, h, g, tm=512):
    s, d = h.shape
    kin = wa.shape[1]
    row = lambda m: (m, 0)
    fixed = lambda m: (0, 0)
    once = lambda m: (layer, 0, 0)
    return pl.pallas_call(
        _gluproj_kernel,
        grid=(s // tm,),
        in_specs=[pl.BlockSpec((tm, kin), row),
                  pl.BlockSpec((None, kin, d), once, pipeline_mode=pl.Buffered(1)),
                  pl.BlockSpec((None, kin, d), once, pipeline_mode=pl.Buffered(1)),
                  pl.BlockSpec((tm, d), row),
                  pl.BlockSpec((1, d), fixed)],
        out_specs=[pl.BlockSpec((tm, d), row), pl.BlockSpec((tm, d), row)],
        out_shape=[jax.ShapeDtypeStruct((s, d), F32), jax.ShapeDtypeStruct((s, d), F32)],
        scratch_shapes=[pltpu.VMEM((kin, d), BF16), pltpu.VMEM((kin, d), BF16)],
        compiler_params=_cparams(("arbitrary",)),
    )(y, wa, wg, h, g.reshape(1, d))


def _swiglu_step(xb, w1_ref, w3_ref, w2_ref):
    a = jnp.dot(xb, w1_ref[...].astype(BF16), preferred_element_type=F32)
    b = jnp.dot(xb, w3_ref[...].astype(BF16), preferred_element_type=F32)
    p = (a * jax.nn.sigmoid(a) * b).astype(BF16)
    return jnp.dot(p, w2_ref[...].astype(BF16), preferred_element_type=F32)


def _ffn_kernel(x_ref, w1_ref, w3_ref, w2_ref, y_ref):
    @pl.when(pl.program_id(1) == 0)
    def _():
        y_ref[...] = jnp.zeros_like(y_ref)

    y_ref[...] += _swiglu_step(x_ref[...], w1_ref, w3_ref, w2_ref)


def ffn(x, w1, w3, w2, layer, tm=1024, tf=256):
    s, d = x.shape
    dff = w1.shape[2]
    row = lambda m, f: (m, 0)
    return pl.pallas_call(
        _ffn_kernel,
        grid=(s // tm, dff // tf),
        in_specs=[pl.BlockSpec((tm, d), row),
                  pl.BlockSpec((None, d, tf), lambda m, f: (layer, 0, f)),
                  pl.BlockSpec((None, d, tf), lambda m, f: (layer, 0, f)),
                  pl.BlockSpec((None, tf, d), lambda m, f: (layer, f, 0))],
        out_specs=pl.BlockSpec((tm, d), row),
        out_shape=jax.ShapeDtypeStruct((s, d), F32),
        compiler_params=_cparams(("parallel", "arbitrary")),
    )(x, w1, w3, w2)


def _ssm_tables(lam_re, lam_im, log_dt, b_re, b_im, c_re, c_im, d_skip, n_doublings):
    L = SSM_CHUNK
    G, P = lam_re.shape
    dt = jnp.exp(log_dt)[:, None]
    mag = jnp.exp(lam_re * dt)
    a_re = mag * jnp.cos(lam_im * dt)
    a_im = mag * jnp.sin(lam_im * dt)
    inv = 1.0 / (lam_re * lam_re + lam_im * lam_im)
    f_re = ((a_re - 1.0) * lam_re + a_im * lam_im) * inv
    f_im = (a_im * lam_re - (a_re - 1.0) * lam_im) * inv
    bb_re = f_re[..., None] * b_re - f_im[..., None] * b_im
    bb_im = f_re[..., None] * b_im + f_im[..., None] * b_re

    pw_re = [jnp.ones_like(a_re)]
    pw_im = [jnp.zeros_like(a_im)]
    for _ in range(L):
        pr, pi = pw_re[-1], pw_im[-1]
        pw_re.append(pr * a_re - pi * a_im)
        pw_im.append(pr * a_im + pi * a_re)
    pw_re = jnp.stack(pw_re)
    pw_im = jnp.stack(pw_im)

    ab_re = pw_re[:L, :, :, None] * bb_re[None] - pw_im[:L, :, :, None] * bb_im[None]
    ab_im = pw_re[:L, :, :, None] * bb_im[None] + pw_im[:L, :, :, None] * bb_re[None]
    hi = lax.Precision.HIGHEST
    kern = (jnp.einsum('gcp,tgpd->tgcd', c_re, ab_re, precision=hi)
            - jnp.einsum('gcp,tgpd->tgcd', c_im, ab_im, precision=hi))
    C = SSM_GROUP
    kern = kern.at[0].add(jnp.eye(C, dtype=F32)[None] * d_skip.reshape(G, C)[:, :, None])
    j_idx = jnp.arange(L)[:, None]
    s_idx = jnp.arange(L)[None, :]
    lag = s_idx - j_idx
    blocks = kern[jnp.clip(lag, 0, L - 1)]
    blocks = jnp.where((lag >= 0)[:, :, None, None, None], blocks, 0.0)
    toep = blocks.transpose(2, 0, 4, 1, 3).reshape(G, L * C, L * C)

    rev_re = ab_re[::-1]
    rev_im = ab_im[::-1]
    bend = jnp.concatenate([rev_re.transpose(1, 0, 3, 2).reshape(G, L * C, P),
                            rev_im.transpose(1, 0, 3, 2).reshape(G, L * C, P)], axis=-1)

    ca_re = c_re[None] * pw_re[1:, :, None, :] - c_im[None] * pw_im[1:, :, None, :]
    ca_im = c_re[None] * pw_im[1:, :, None, :] + c_im[None] * pw_re[1:, :, None, :]
    cdec = jnp.concatenate([ca_re.transpose(1, 3, 0, 2).reshape(G, P, L * C),
                            (-ca_im).transpose(1, 3, 0, 2).reshape(G, P, L * C)], axis=1)

    sq_re, sq_im = pw_re[L], pw_im[L]
    same, cross = [], []
    for _ in range(n_doublings):
        same.append(jnp.concatenate([sq_re, sq_re], axis=-1))
        cross.append(jnp.concatenate([-sq_im, sq_im], axis=-1))
        sq_re, sq_im = sq_re * sq_re - sq_im * sq_im, 2.0 * sq_re * sq_im
    return toep, bend, cdec, jnp.stack(same, axis=1), jnp.stack(cross, axis=1)


def _ssm_kernel(u_ref, toep_ref, bend_ref, cdec_ref, same_ref, cross_ref, y_ref):
    u = u_ref[...].astype(BF16)
    nc = u.shape[0]
    x = jnp.dot(u, bend_ref[...].astype(BF16), preferred_element_type=F32)
    row = lax.broadcasted_iota(jnp.int32, x.shape, 0)
    for k in range(same_ref.shape[0]):
        shift = 1 << k
        sh = jnp.where(row >= shift, pltpu.roll(x, shift, 0), 0.0)
        x = x + same_ref[k:k + 1, :] * sh + cross_ref[k:k + 1, :] * pltpu.roll(sh, SSM_STATE, 1)
    prev = jnp.where(row >= 1, pltpu.roll(x, 1, 0), 0.0).astype(BF16)
    y = jnp.dot(u, toep_ref[...].astype(BF16), preferred_element_type=F32)
    y = y + jnp.dot(prev, cdec_ref[...].astype(BF16), preferred_element_type=F32)
    y_ref[...] = jax.nn.gelu(y).astype(y_ref.dtype)


def ssm_core(u, params):
    s = u.shape[0]
    L, C, P = SSM_CHUNK, SSM_GROUP, SSM_STATE
    nc = s // L
    n_doublings = max(1, (nc - 1).bit_length())
    toep, bend, cdec, same, cross = _ssm_tables(*params, n_doublings=n_doublings)
    G, LC, _ = toep.shape
    ug = u.reshape(nc, L, G, C).transpose(2, 0, 1, 3).reshape(G, nc, LC)
    grp = lambda g: (g, 0, 0)
    y = pl.pallas_call(
        _ssm_kernel,
        grid=(G,),
        in_specs=[pl.BlockSpec((None, nc, LC), grp),
                  pl.BlockSpec((None, LC, LC), grp),
                  pl.BlockSpec((None, LC, 2 * P), grp),
                  pl.BlockSpec((None, 2 * P, LC), grp),
                  pl.BlockSpec((None, n_doublings, 2 * P), grp),
                  pl.BlockSpec((None, n_doublings, 2 * P), grp)],
        out_specs=pl.BlockSpec((None, nc, LC), grp),
        out_shape=jax.ShapeDtypeStruct((G, nc, LC), F32),
        compiler_params=_cparams(("parallel",)),
    )(ug, toep, bend, cdec, same, cross)
    y = y.reshape(G, nc, L, C).transpose(1, 2, 0, 3).reshape(s, G * C)
    return y.astype(BF16)


def _router_kernel(x_ref, r_ref, o_ref):
    logits = jnp.dot(x_ref[...], r_ref[...], preferred_element_type=F32, precision=lax.Precision.HIGHEST)
    lane = lax.broadcasted_iota(jnp.int32, logits.shape, 1)
    logits = jnp.where(lane < N_EXPERTS, logits, -jnp.inf)
    big = jnp.int32(logits.shape[1])
    v1 = jnp.max(logits, axis=-1, keepdims=True)
    i1 = jnp.min(jnp.where(logits == v1, lane, big), axis=-1, keepdims=True)
    rest = jnp.where(lane == i1, -jnp.inf, logits)
    v2 = jnp.max(rest, axis=-1, keepdims=True)
    i2 = jnp.min(jnp.where(rest == v2, lane, big), axis=-1, keepdims=True)
    e2 = jnp.exp(v2 - v1)
    g1 = 1.0 / (1.0 + e2)
    g2 = e2 / (1.0 + e2)
    out = jnp.where(lane == 0, i1.astype(F32), 0.0)
    out = jnp.where(lane == 1, i2.astype(F32), out)
    out = jnp.where(lane == 2, g1, out)
    out = jnp.where(lane == 3, g2, out)
    o_ref[...] = out


def router_top2(x, router, tm=512):
    s, d = x.shape
    lanes = 128
    rp = jnp.zeros((d, lanes), F32).at[:, :N_EXPERTS].set(router)
    return pl.pallas_call(
        _router_kernel,
        grid=(s // tm,),
        in_specs=[pl.BlockSpec((tm, d), lambda m: (m, 0)),
                  pl.BlockSpec((d, lanes), lambda m: (0, 0))],
        out_specs=pl.BlockSpec((tm, lanes), lambda m: (m, 0)),
        out_shape=jax.ShapeDtypeStruct((s, lanes), F32),
        compiler_params=_cparams(("parallel",)),
    )(x, rp)


def _moe_kernel(texp_ref, trows_ref, perm_ref,
                x_hbm, w1_ref, w3_ref, w2_ref, y_hbm, xb, acc, gsem, ssem):
    t = pl.program_id(0)
    f = pl.program_id(1)
    tm = acc.shape[0]
    n_tok = x_hbm.shape[0]
    rows = trows_ref[t]
    base = t * tm

    def gather_copy(i):
        a = perm_ref[base + i]
        tok = jnp.where(a >= n_tok, a - n_tok, a)
        return pltpu.make_async_copy(x_hbm.at[pl.ds(tok, 1), :], acc.at[pl.ds(i, 1), :], gsem)

    def scatter_copy(i):
        return pltpu.make_async_copy(acc.at[pl.ds(i, 1), :], y_hbm.at[pl.ds(perm_ref[base + i], 1), :], ssem)

    def for_rows(fn):
        groups = rows // DMA_UNROLL

        def group_body(gi, carry):
            for u in range(DMA_UNROLL):
                fn(gi * DMA_UNROLL + u)
            return carry

        def row_body(i, carry):
            fn(i)
            return carry

        lax.fori_loop(0, groups, group_body, 0)
        lax.fori_loop(groups * DMA_UNROLL, rows, row_body, 0)

    def start_all(make):
        for_rows(lambda i: make(i).start())

    def wait_all(make):
        for_rows(lambda i: make(i).wait())

    @pl.when(rows > 0)
    def _():
        @pl.when(f == 0)
        def _():
            acc[...] = jnp.zeros_like(acc)
            start_all(gather_copy)
            wait_all(gather_copy)
            xb[...] = acc[...].astype(BF16)
            acc[...] = jnp.zeros_like(acc)

        acc[...] += _swiglu_step(xb[...], w1_ref, w3_ref, w2_ref)

        @pl.when(f == pl.num_programs(1) - 1)
        def _():
            start_all(scatter_copy)
            wait_all(scatter_copy)


def moe_experts(x, w1, w3, w2, layer, tile_expert, tile_rows, perm, tm, tf=256):
    s, d = x.shape
    dff = w1.shape[3]
    n_tiles = tile_expert.shape[0]
    nf = dff // tf
    last = nf - 1

    def fsel(t, f, trows):
        return jnp.where(trows[t] > 0, f, last)

    grid_spec = pltpu.PrefetchScalarGridSpec(
        num_scalar_prefetch=3,
        grid=(n_tiles, nf),
        in_specs=[pl.BlockSpec(memory_space=pl.ANY),
                  pl.BlockSpec((None, None, d, tf), lambda t, f, te, tr, pm: (layer, te[t], 0, fsel(t, f, tr))),
                  pl.BlockSpec((None, None, d, tf), lambda t, f, te, tr, pm: (layer, te[t], 0, fsel(t, f, tr))),
                  pl.BlockSpec((None, None, tf, d), lambda t, f, te, tr, pm: (layer, te[t], fsel(t, f, tr), 0))],
        out_specs=pl.BlockSpec(memory_space=pl.ANY),
        scratch_shapes=[pltpu.VMEM((tm, d), BF16),
                        pltpu.VMEM((tm, d), F32),
                        pltpu.SemaphoreType.DMA(()),
                        pltpu.SemaphoreType.DMA(())],
    )
    return pl.pallas_call(
        _moe_kernel,
        grid_spec=grid_spec,
        out_shape=jax.ShapeDtypeStruct((TOP_K * s, d), F32),
        compiler_params=_cparams(("arbitrary", "arbitrary")),
    )(tile_expert, tile_rows, perm, x, w1, w3, w2)


def _route_plan(e1, e2, tm, n_tiles):
    s = e1.shape[0]
    experts = jnp.concatenate([e1, e2])
    onehot = (experts[:, None] == jnp.arange(N_EXPERTS, dtype=jnp.int32)[None, :]).astype(jnp.int32)
    csum = jnp.cumsum(onehot, axis=0)
    counts = csum[-1]
    tiles_per = (counts + tm - 1) // tm
    tile_end = jnp.cumsum(tiles_per)
    tile_start = tile_end - tiles_per
    pos = jnp.sum(onehot * (tile_start[None, :] * tm + csum - 1), axis=1)
    n_rows = n_tiles * tm
    perm = jnp.zeros((n_rows,), jnp.int32).at[pos].set(jnp.arange(TOP_K * s, dtype=jnp.int32))
    tid = jnp.arange(n_tiles, dtype=jnp.int32)
    texp = jnp.minimum(jnp.sum((tid[:, None] >= tile_end[None, :]).astype(jnp.int32), axis=1), N_EXPERTS - 1)
    used = tid < tile_end[-1]
    local = tid - tile_start[texp]
    trows = jnp.where(used, jnp.clip(counts[texp] - local * tm, 0, tm), 0).astype(jnp.int32)
    last_used = texp[jnp.maximum(tile_end[-1] - 1, 0)]
    texp = jnp.where(used, texp, last_used).astype(jnp.int32)
    return texp, trows, perm


def _combine_kernel(h_ref, y1_ref, y2_ref, gt_ref, g_ref, *out_refs, want_h):
    gates = gt_ref[...]
    hnew = h_ref[...] + gates[:, 2:3] * y1_ref[...] + gates[:, 3:4] * y2_ref[...]
    if want_h:
        out_refs[0][...] = hnew
    out_refs[-1][...] = _rms(hnew, g_ref[...]).astype(out_refs[-1].dtype)


def combine_residual_norm(h, y, route, g, want_h, norm_dtype, tm=512):
    s, d = h.shape
    nb = s // tm
    row = lambda m: (m, 0)
    outs_spec = [pl.BlockSpec((tm, d), row)]
    outs_shape = [jax.ShapeDtypeStruct((s, d), norm_dtype)]
    if want_h:
        outs_spec = [pl.BlockSpec((tm, d), row)] + outs_spec
        outs_shape = [jax.ShapeDtypeStruct((s, d), F32)] + outs_shape
    return pl.pallas_call(
        functools.partial(_combine_kernel, want_h=want_h),
        grid=(nb,),
        in_specs=[pl.BlockSpec((tm, d), row),
                  pl.BlockSpec((tm, d), row),
                  pl.BlockSpec((tm, d), lambda m: (m + nb, 0)),
                  pl.BlockSpec((tm, route.shape[1]), row),
                  pl.BlockSpec((1, d), lambda m: (0, 0))],
        out_specs=outs_spec,
        out_shape=outs_shape,
        compiler_params=_cparams(("parallel",)),
    )(h, y, y, route, g.reshape(1, d))


MOE_TM = 1024


def kernel(x, even_norm1, even_w_in, even_conv_w, even_conv_b, even_cln_g, even_cln_b, even_w_out, even_norm2, even_ffn_w1, even_ffn_w3, even_ffn_w2, odd_norm1, odd_ssm_w_in, odd_lam_re, odd_lam_im, odd_log_dt, odd_b_re, odd_b_im, odd_c_re, odd_c_im, odd_d_skip, odd_glu_wa, odd_glu_wg, odd_norm2, odd_router, odd_moe_w1, odd_moe_w3, odd_moe_w2, final_norm):
    b, s, d = x.shape
    assert b == 1
    depth = even_norm1.shape[0] + odd_norm1.shape[0]
    assert even_norm1.shape[0] == odd_norm1.shape[0]
    h = x.reshape(s, d)
    hn = rmsnorm(h, even_norm1[0], BF16)
    n_tiles = (TOP_K * s) // MOE_TM + N_EXPERTS
    out = None
    ffn_out = None
    for layer in range(depth):
        j = layer // 2
        if layer % 2 == 0:
            proj = matmul(hn, even_w_in, j)
            att = dilated_attention(proj)
            cv = conv_module(proj, even_conv_w[j], even_conv_b[j], even_cln_g[j], even_cln_b[j])
            h, hn = outproj_residual_norm(att, cv, even_w_out, j, h, even_norm2[j])
            ffn_out = ffn(hn, even_ffn_w1, even_ffn_w3, even_ffn_w2, j)
        else:
            h, u = residual_norm_matmul(h, ffn_out, odd_norm1[j], odd_ssm_w_in, j)
            y = ssm_core(u, (odd_lam_re[j], odd_lam_im[j], odd_log_dt[j], odd_b_re[j], odd_b_im[j],
                             odd_c_re[j], odd_c_im[j], odd_d_skip[j]))
            h, hn = gluproj_residual_norm(y, odd_glu_wa, odd_glu_wg, j, h, odd_norm2[j])
            route = router_top2(hn, odd_router[j])
            e1 = route[:, 0].astype(jnp.int32)
            e2 = route[:, 1].astype(jnp.int32)
            texp, trows, perm = _route_plan(e1, e2, MOE_TM, n_tiles)
            ys = moe_experts(hn, odd_moe_w1, odd_moe_w3, odd_moe_w2, j, texp, trows, perm, MOE_TM)
            if layer + 1 < depth:
                h, hn = combine_residual_norm(h, ys, route, even_norm1[j + 1], True, BF16)
            else:
                (out,) = combine_residual_norm(h, ys, route, final_norm, False, F32)
    return out.reshape(b, s, d)
```

```python
import functools
import math

import jax
import jax.numpy as jnp
from jax import lax
from jax.experimental import pallas as pl
from jax.experimental.pallas import tpu as pltpu

F32 = jnp.float32
BF16 = jnp.bfloat16

NORM_EPS = 1e-5
N_HEADS = 8
HEAD_DIM = 128
D_ATT = N_HEADS * HEAD_DIM
DILATIONS = (1, 4, 16)
BAND = 128
ATT_CHUNK = BAND * DILATIONS[-1]
ATT_UNROLL = 4
ALIBI_MAX_BIAS = 8.0
D_CONV = 1024
CONV_WIDTH = 31
CONV_HALO = 32
SSM_GROUP = 16
SSM_STATE = 64
SSM_CHUNK = 16
SSM_LANES = 128
N_EXPERTS = 8
TOP_K = 2
DMA_UNROLL = 8

VMEM_LIMIT = 56 * 1024 * 1024


def _cparams(sem):
    return pltpu.CompilerParams(dimension_semantics=sem, vmem_limit_bytes=VMEM_LIMIT)


def _rms(x, g):
    return x * lax.rsqrt(jnp.mean(x * x, axis=-1, keepdims=True) + NORM_EPS) * g


def _rmsnorm_kernel(x_ref, g_ref, o_ref):
    o_ref[...] = _rms(x_ref[...], g_ref[...]).astype(o_ref.dtype)


def rmsnorm(x, g, out_dtype, tm=512):
    s, d = x.shape
    return pl.pallas_call(
        _rmsnorm_kernel,
        grid=(s // tm,),
        in_specs=[pl.BlockSpec((tm, d), lambda m: (m, 0)),
                  pl.BlockSpec((1, d), lambda m: (0, 0))],
        out_specs=pl.BlockSpec((tm, d), lambda m: (m, 0)),
        out_shape=jax.ShapeDtypeStruct((s, d), out_dtype),
        compiler_params=_cparams(("parallel",)),
    )(x, g.reshape(1, d))


def _matmul_kernel(x_ref, w_ref, o_ref, wb_ref):
    @pl.when(pl.program_id(1) == 0)
    def _():
        wb_ref[...] = w_ref[...].astype(BF16)

    o_ref[...] = jnp.dot(x_ref[...], wb_ref[...], preferred_element_type=F32).astype(o_ref.dtype)


def matmul(x, w, layer, out_dtype=F32, tm=512, tn=1024):
    s, k = x.shape
    n = w.shape[2]
    tn = min(tn, n)
    return pl.pallas_call(
        _matmul_kernel,
        grid=(n // tn, s // tm),
        in_specs=[pl.BlockSpec((tm, k), lambda j, m: (m, 0)),
                  pl.BlockSpec((None, k, tn), lambda j, m: (layer, 0, j))],
        out_specs=pl.BlockSpec((tm, tn), lambda j, m: (m, j)),
        out_shape=jax.ShapeDtypeStruct((s, n), out_dtype),
        scratch_shapes=[pltpu.VMEM((k, tn), BF16)],
        compiler_params=_cparams(("arbitrary", "arbitrary")),
    )(x, w)


def _resnorm_matmul_kernel(h_ref, y_ref, g_ref, w_ref, hnew_ref, o_ref, wb_ref):
    @pl.when(pl.program_id(0) == 0)
    def _():
        wb_ref[...] = w_ref[...].astype(BF16)

    hnew = h_ref[...] + y_ref[...]
    hnew_ref[...] = hnew
    xn = _rms(hnew, g_ref[...]).astype(BF16)
    o_ref[...] = jnp.dot(xn, wb_ref[...], preferred_element_type=F32).astype(o_ref.dtype)


def residual_norm_matmul(h, y, g, w, layer, tm=256):
    s, d = h.shape
    n = w.shape[2]
    row = lambda m: (m, 0)
    return pl.pallas_call(
        _resnorm_matmul_kernel,
        grid=(s // tm,),
        in_specs=[pl.BlockSpec((tm, d), row),
                  pl.BlockSpec((tm, d), row),
                  pl.BlockSpec((1, d), lambda m: (0, 0)),
                  pl.BlockSpec((None, d, n), lambda m: (layer, 0, 0))],
        out_specs=[pl.BlockSpec((tm, d), row), pl.BlockSpec((tm, n), row)],
        out_shape=[jax.ShapeDtypeStruct((s, d), F32), jax.ShapeDtypeStruct((s, n), F32)],
        scratch_shapes=[pltpu.VMEM((d, n), BF16)],
        compiler_params=_cparams(("arbitrary",)),
    )(h, y, g.reshape(1, d), w)


def _attn_kernel(q_ref, kp_ref, kc_ref, vp_ref, vc_ref, o_ref, kk, vv, osc, lsc):
    h = pl.program_id(0)
    c = pl.program_id(1)
    kk[0:ATT_CHUNK, :] = kp_ref[...]
    kk[ATT_CHUNK:2 * ATT_CHUNK, :] = kc_ref[...]
    vv[0:ATT_CHUNK, :] = vp_ref[...]
    vv[ATT_CHUNK:2 * ATT_CHUNK, :] = vc_ref[...]

    qi = lax.broadcasted_iota(jnp.int32, (BAND, 2 * BAND), 0)
    kj = lax.broadcasted_iota(jnp.int32, (BAND, 2 * BAND), 1)
    dist = BAND + qi - kj
    band = (dist >= 0) & (dist <= BAND)
    head = jnp.full((BAND, 2 * BAND), h + 1, jnp.int32).astype(F32)
    slope = jnp.exp2(-(ALIBI_MAX_BIAS / N_HEADS) * head)
    sdist = slope * dist.astype(F32)
    scale = HEAD_DIM ** -0.5
    blocks = ATT_CHUNK // BAND

    for g, d in enumerate(DILATIONS):
        bias = jnp.where(band, -sdist * float(d), -jnp.inf)
        bias_first = jnp.where(kj >= BAND, bias, -jnp.inf)

        def one_block(b, g=g, d=d, bias=bias, bias_first=bias_first):
            n = b // d
            r = b % d
            qstart = n * (BAND * d) + r
            kstart = ATT_CHUNK + (n - 1) * (BAND * d) + r
            q = q_ref[pl.ds(qstart, BAND, stride=d), :].astype(BF16)
            k = kk[pl.ds(kstart, 2 * BAND, stride=d), :].astype(BF16)
            v = vv[pl.ds(kstart, 2 * BAND, stride=d), :].astype(BF16)
            s = lax.dot_general(q, k, (((1,), (1,)), ((), ())), preferred_element_type=F32) * scale
            s = s + jnp.where((c == 0) & (n == 0), bias_first, bias)
            m = jnp.max(s, axis=-1, keepdims=True)
            p = jnp.exp(s - m)
            den = jnp.sum(p, axis=-1, keepdims=True)
            o = jnp.dot(p.astype(BF16), v, preferred_element_type=F32) / den
            lse = m + jnp.log(den)
            osc[g, pl.ds(qstart, BAND, stride=d), :] = o
            lsc[g, pl.ds(qstart, BAND, stride=d), :] = jnp.broadcast_to(lse, (BAND, HEAD_DIM))

        def body(i, carry, one_block=one_block):
            for u in range(ATT_UNROLL):
                one_block(i * ATT_UNROLL + u)
            return carry

        lax.fori_loop(0, blocks // ATT_UNROLL, body, 0)

    l0, l1, l2 = lsc[0], lsc[1], lsc[2]
    m = jnp.maximum(jnp.maximum(l0, l1), l2)
    w0, w1, w2 = jnp.exp(l0 - m), jnp.exp(l1 - m), jnp.exp(l2 - m)
    att = (w0 * osc[0] + w1 * osc[1] + w2 * osc[2]) / (w0 + w1 + w2)
    o_ref[...] = att.astype(o_ref.dtype)


def dilated_attention(proj):
    s = proj.shape[0]
    blk = (ATT_CHUNK, HEAD_DIM)
    prev = lambda c: jnp.maximum(c - 1, 0)
    return pl.pallas_call(
        _attn_kernel,
        grid=(N_HEADS, s // ATT_CHUNK),
        in_specs=[pl.BlockSpec(blk, lambda h, c: (c, h)),
                  pl.BlockSpec(blk, lambda h, c: (prev(c), N_HEADS + h)),
                  pl.BlockSpec(blk, lambda h, c: (c, N_HEADS + h)),
                  pl.BlockSpec(blk, lambda h, c: (prev(c), 2 * N_HEADS + h)),
                  pl.BlockSpec(blk, lambda h, c: (c, 2 * N_HEADS + h))],
        out_specs=pl.BlockSpec(blk, lambda h, c: (c, h)),
        out_shape=jax.ShapeDtypeStruct((s, D_ATT), BF16),
        scratch_shapes=[pltpu.VMEM((2 * ATT_CHUNK, HEAD_DIM), F32),
                        pltpu.VMEM((2 * ATT_CHUNK, HEAD_DIM), F32),
                        pltpu.VMEM((len(DILATIONS), ATT_CHUNK, HEAD_DIM), F32),
                        pltpu.VMEM((len(DILATIONS), ATT_CHUNK, HEAD_DIM), F32)],
        compiler_params=_cparams(("parallel", "parallel")),
    )(proj, proj, proj, proj, proj)


def _conv_kernel(val_ref, gate_ref, hval_ref, hgate_ref, w_ref, b_ref, lg_ref, lb_ref, o_ref, glu):
    m = pl.program_id(0)
    tm = val_ref.shape[0]
    halo = hval_ref[...] * jax.nn.sigmoid(hgate_ref[...])
    glu[0:CONV_HALO, :] = jnp.where(m > 0, halo, 0.0)
    glu[CONV_HALO:CONV_HALO + tm, :] = val_ref[...] * jax.nn.sigmoid(gate_ref[...])
    first = CONV_HALO - (CONV_WIDTH - 1)

    acc = jnp.broadcast_to(b_ref[...], (tm, D_CONV))
    for j in range(CONV_WIDTH):
        acc = acc + w_ref[j:j + 1, :] * glu[pl.ds(first + j, tm), :]
    mu = jnp.mean(acc, axis=-1, keepdims=True)
    cen = acc - mu
    var = jnp.mean(cen * cen, axis=-1, keepdims=True)
    y = cen * lax.rsqrt(var + NORM_EPS) * lg_ref[...] + lb_ref[...]
    o_ref[...] = (y * jax.nn.sigmoid(y)).astype(o_ref.dtype)


def conv_module(proj, conv_w, conv_b, ln_g, ln_b, tm=512):
    s = proj.shape[0]
    vcol = 3 * D_ATT // D_CONV
    gcol = vcol + 1
    per = tm // CONV_HALO
    hrow = lambda m: jnp.maximum(m * per - 1, 0)
    vec = pl.BlockSpec((1, D_CONV), lambda m: (0, 0))
    return pl.pallas_call(
        _conv_kernel,
        grid=(s // tm,),
        in_specs=[pl.BlockSpec((tm, D_CONV), lambda m: (m, vcol)),
                  pl.BlockSpec((tm, D_CONV), lambda m: (m, gcol)),
                  pl.BlockSpec((CONV_HALO, D_CONV), lambda m: (hrow(m), vcol)),
                  pl.BlockSpec((CONV_HALO, D_CONV), lambda m: (hrow(m), gcol)),
                  pl.BlockSpec((CONV_WIDTH, D_CONV), lambda m: (0, 0)),
                  vec, vec, vec],
        out_specs=pl.BlockSpec((tm, D_CONV), lambda m: (m, 0)),
        out_shape=jax.ShapeDtypeStruct((s, D_CONV), BF16),
        scratch_shapes=[pltpu.VMEM((CONV_HALO + tm, D_CONV), F32)],
        compiler_params=_cparams(("parallel",)),
    )(proj, proj, proj, proj, conv_w, conv_b.reshape(1, -1), ln_g.reshape(1, -1), ln_b.reshape(1, -1))


def _outproj_kernel(att_ref, cv_ref, w_ref, h_ref, g_ref, hnew_ref, hn_ref, wb_ref):
    @pl.when(pl.program_id(0) == 0)
    def _():
        wb_ref[...] = w_ref[...].astype(BF16)

    y = jnp.dot(att_ref[...], wb_ref[0:D_ATT, :], preferred_element_type=F32)
    y = y + jnp.dot(cv_ref[...], wb_ref[D_ATT:D_ATT + D_CONV, :], preferred_element_type=F32)
    hnew = h_ref[...] + y
    hnew_ref[...] = hnew
    hn_ref[...] = _rms(hnew, g_ref[...]).astype(hn_ref.dtype)


def outproj_residual_norm(att, cv, w_out, layer, h, g, tm=512):
    s, d = h.shape
    kin = w_out.shape[1]
    row = lambda m: (m, 0)
    fixed = lambda m: (0, 0)
    return pl.pallas_call(
        _outproj_kernel,
        grid=(s // tm,),
        in_specs=[pl.BlockSpec((tm, D_ATT), row),
                  pl.BlockSpec((tm, D_CONV), row),
                  pl.BlockSpec((None, kin, d), lambda m: (layer, 0, 0), pipeline_mode=pl.Buffered(1)),
                  pl.BlockSpec((tm, d), row),
                  pl.BlockSpec((1, d), fixed)],
        out_specs=[pl.BlockSpec((tm, d), row), pl.BlockSpec((tm, d), row)],
        out_shape=[jax.ShapeDtypeStruct((s, d), F32), jax.ShapeDtypeStruct((s, d), BF16)],
        scratch_shapes=[pltpu.VMEM((kin, d), BF16)],
        compiler_params=_cparams(("arbitrary",)),
    )(att, cv, w_out, h, g.reshape(1, d))


def _gluproj_kernel(y_ref, wa_ref, wg_ref, h_ref, g_ref, hnew_ref, hn_ref, wab_ref, wgb_ref):
    @pl.when(pl.program_id(0) == 0)
    def _():
        wab_ref[...] = wa_ref[...].astype(BF16)
        wgb_ref[...] = wg_ref[...].astype(BF16)

    y = y_ref[...].astype(BF16)
    a = jnp.dot(y, wab_ref[...], preferred_element_type=F32)
    gt = jnp.dot(y, wgb_ref[...], preferred_element_type=F32)
    hnew = h_ref[...] + a * jax.nn.sigmoid(gt)
    hnew_ref[...] = hnew
    hn_ref[...] = _rms(hnew, g_ref[...]).astype(hn_ref.dtype)


def gluproj_residual_norm(y, wa, wg, layer, h, g, tm=256):
    s, d = h.shape
    kin = wa.shape[1]
    row = lambda m: (m, 0)
    fixed = lambda m: (0, 0)
    once = lambda m: (layer, 0, 0)
    return pl.pallas_call(
        _gluproj_kernel,
        grid=(s // tm,),
        in_specs=[pl.BlockSpec((tm, kin), row),
                  pl.BlockSpec((None, kin, d), once, pipeline_mode=pl.Buffered(1)),
                  pl.BlockSpec((None, kin, d), once, pipeline_mode=pl.Buffered(1)),
                  pl.BlockSpec((tm, d), row),
                  pl.BlockSpec((1, d), fixed)],
        out_specs=[pl.BlockSpec((tm, d), row), pl.BlockSpec((tm, d), row)],
        out_shape=[jax.ShapeDtypeStruct((s, d), F32), jax.ShapeDtypeStruct((s, d), F32)],
        scratch_shapes=[pltpu.VMEM((kin, d), BF16), pltpu.VMEM((kin, d), BF16)],
        compiler_params=_cparams(("arbitrary",)),
    )(y, wa, wg, h, g.reshape(1, d))


def _swiglu_step(xb, w1_ref, w3_ref, w2_ref):
    a = jnp.dot(xb, w1_ref[...].astype(BF16), preferred_element_type=F32)
    b = jnp.dot(xb, w3_ref[...].astype(BF16), preferred_element_type=F32)
    p = (a * jax.nn.sigmoid(a) * b).astype(BF16)
    return jnp.dot(p, w2_ref[...].astype(BF16), preferred_element_type=F32)


def _ffn_kernel(x_ref, w1_ref, w3_ref, w2_ref, y_ref):
    @pl.when(pl.program_id(1) == 0)
    def _():
        y_ref[...] = jnp.zeros_like(y_ref)

    y_ref[...] += _swiglu_step(x_ref[...], w1_ref, w3_ref, w2_ref)


def ffn(x, w1, w3, w2, layer, tm=1024, tf=256):
    s, d = x.shape
    dff = w1.shape[2]
    row = lambda m, f: (m, 0)
    return pl.pallas_call(
        _ffn_kernel,
        grid=(s // tm, dff // tf),
        in_specs=[pl.BlockSpec((tm, d), row),
                  pl.BlockSpec((None, d, tf), lambda m, f: (layer, 0, f)),
                  pl.BlockSpec((None, d, tf), lambda m, f: (layer, 0, f)),
                  pl.BlockSpec((None, tf, d), lambda m, f: (layer, f, 0))],
        out_specs=pl.BlockSpec((tm, d), row),
        out_shape=jax.ShapeDtypeStruct((s, d), F32),
        compiler_params=_cparams(("parallel", "arbitrary")),
    )(x, w1, w3, w2)


def _ssm_tables(lam_re, lam_im, log_dt, b_re, b_im, c_re, c_im, d_skip, n_doublings):
    L = SSM_CHUNK
    G, P = lam_re.shape
    dt = jnp.exp(log_dt)[:, None]
    mag = jnp.exp(lam_re * dt)
    a_re = mag * jnp.cos(lam_im * dt)
    a_im = mag * jnp.sin(lam_im * dt)
    inv = 1.0 / (lam_re * lam_re + lam_im * lam_im)
    f_re = ((a_re - 1.0) * lam_re + a_im * lam_im) * inv
    f_im = (a_im * lam_re - (a_re - 1.0) * lam_im) * inv
    bb_re = f_re[..., None] * b_re - f_im[..., None] * b_im
    bb_im = f_re[..., None] * b_im + f_im[..., None] * b_re

    pw_re = [jnp.ones_like(a_re)]
    pw_im = [jnp.zeros_like(a_im)]
    for _ in range(L):
        pr, pi = pw_re[-1], pw_im[-1]
        pw_re.append(pr * a_re - pi * a_im)
        pw_im.append(pr * a_im + pi * a_re)
    pw_re = jnp.stack(pw_re)
    pw_im = jnp.stack(pw_im)

    ab_re = pw_re[:L, :, :, None] * bb_re[None] - pw_im[:L, :, :, None] * bb_im[None]
    ab_im = pw_re[:L, :, :, None] * bb_im[None] + pw_im[:L, :, :, None] * bb_re[None]
    hi = lax.Precision.HIGHEST
    kern = (jnp.einsum('gcp,tgpd->tgcd', c_re, ab_re, precision=hi)
            - jnp.einsum('gcp,tgpd->tgcd', c_im, ab_im, precision=hi))
    C = SSM_GROUP
    Q = SSM_LANES // C
    O = G // Q
    kern = kern.at[0].add(jnp.eye(C, dtype=F32)[None] * d_skip.reshape(G, C)[:, :, None])
    same_group = jnp.eye(Q, dtype=F32)

    kr = kern.reshape(L, O, Q, C, C).transpose(1, 2, 4, 0, 3)
    kcat = (kr[:, :, :, :, None, :] * same_group[None, :, None, None, :, None]).reshape(O, Q * C, L * Q * C)

    def spread_b(ab):
        r = ab[::-1].reshape(L, O, Q, P, C).transpose(1, 0, 2, 4, 3)
        return r[:, :, :, :, None, :] * same_group[None, None, :, None, :, None]
    bpow = jnp.stack([spread_b(ab_re), spread_b(ab_im)], axis=4).reshape(O, L, Q * C, 2 * Q * P)

    ca_re = c_re[None] * pw_re[1:, :, None, :] - c_im[None] * pw_im[1:, :, None, :]
    ca_im = c_re[None] * pw_im[1:, :, None, :] + c_im[None] * pw_re[1:, :, None, :]
    def spread_c(ca):
        r = ca.reshape(L, O, Q, C, P).transpose(1, 2, 4, 0, 3)
        return r[:, :, :, :, None, :] * same_group[None, :, None, None, :, None]
    cdec = jnp.stack([spread_c(ca_re), spread_c(-ca_im)], axis=1).reshape(O, 2 * Q * P, L * Q * C)

    sq_re, sq_im = pw_re[L].reshape(O, Q * P), pw_im[L].reshape(O, Q * P)
    same, cross = [], []
    for _ in range(n_doublings):
        same.append(jnp.concatenate([sq_re, sq_re], axis=-1))
        cross.append(jnp.concatenate([-sq_im, sq_im], axis=-1))
        sq_re, sq_im = sq_re * sq_re - sq_im * sq_im, 2.0 * sq_re * sq_im
    return (kcat.astype(BF16), bpow.astype(BF16), cdec.astype(BF16),
            jnp.stack(same, axis=1), jnp.stack(cross, axis=1))


def _ssm_kernel(u_ref, kcat_ref, bpow_ref, cdec_ref, same_ref, cross_ref, y_ref, yacc):
    L = SSM_CHUNK
    nc = yacc.shape[0]
    half = same_ref.shape[1] // 2
    yacc[...] = jnp.zeros_like(yacc)
    x = jnp.zeros((nc, same_ref.shape[1]), F32)
    for j in range(L):
        uj = u_ref[pl.ds(j, nc, stride=L), :].astype(BF16)
        x = x + jnp.dot(uj, bpow_ref[j], preferred_element_type=F32)
        width = (L - j) * SSM_LANES
        yacc[:, j * SSM_LANES:] += jnp.dot(uj, kcat_ref[:, :width], preferred_element_type=F32)
    row = lax.broadcasted_iota(jnp.int32, x.shape, 0)
    for k in range(same_ref.shape[0]):
        shift = 1 << k
        sh = jnp.where(row >= shift, pltpu.roll(x, shift, 0), 0.0)
        x = x + same_ref[k:k + 1, :] * sh + cross_ref[k:k + 1, :] * pltpu.roll(sh, half, 1)
    prev = jnp.where(row >= 1, pltpu.roll(x, 1, 0), 0.0).astype(BF16)
    yacc[...] += jnp.dot(prev, cdec_ref[...], preferred_element_type=F32)
    for s in range(L):
        y_ref[pl.ds(s, nc, stride=L), :] = jax.nn.gelu(yacc[:, s * SSM_LANES:(s + 1) * SSM_LANES])


def ssm_core(u, params):
    s, width = u.shape
    L = SSM_CHUNK
    nc = s // L
    n_doublings = max(1, (nc - 1).bit_length())
    kcat, bpow, cdec, same, cross = _ssm_tables(*params, n_doublings=n_doublings)
    n_blocks = width // SSM_LANES
    n_state = same.shape[2]
    blk = lambda o: (o, 0, 0)
    return pl.pallas_call(
        _ssm_kernel,
        grid=(n_blocks,),
        in_specs=[pl.BlockSpec((s, SSM_LANES), lambda o: (0, o)),
                  pl.BlockSpec((None, SSM_LANES, L * SSM_LANES), blk),
                  pl.BlockSpec((None, L, SSM_LANES, n_state), lambda o: (o, 0, 0, 0)),
                  pl.BlockSpec((None, n_state, L * SSM_LANES), blk),
                  pl.BlockSpec((None, n_doublings, n_state), blk),
                  pl.BlockSpec((None, n_doublings, n_state), blk)],
        out_specs=pl.BlockSpec((s, SSM_LANES), lambda o: (0, o)),
        out_shape=jax.ShapeDtypeStruct((s, width), F32),
        scratch_shapes=[pltpu.VMEM((nc, L * SSM_LANES), F32)],
        compiler_params=_cparams(("parallel",)),
    )(u, kcat, bpow, cdec, same, cross)


def _router_kernel(x_ref, r_ref, o_ref):
    logits = jnp.dot(x_ref[...], r_ref[...], preferred_element_type=F32, precision=lax.Precision.HIGHEST)
    lane = lax.broadcasted_iota(jnp.int32, logits.shape, 1)
    logits = jnp.where(lane < N_EXPERTS, logits, -jnp.inf)
    big = jnp.int32(logits.shape[1])
    v1 = jnp.max(logits, axis=-1, keepdims=True)
    i1 = jnp.min(jnp.where(logits == v1, lane, big), axis=-1, keepdims=True)
    rest = jnp.where(lane == i1, -jnp.inf, logits)
    v2 = jnp.max(rest, axis=-1, keepdims=True)
    i2 = jnp.min(jnp.where(rest == v2, lane, big), axis=-1, keepdims=True)
    e2 = jnp.exp(v2 - v1)
    g1 = 1.0 / (1.0 + e2)
    g2 = e2 / (1.0 + e2)
    out = jnp.where(lane == 0, i1.astype(F32), 0.0)
    out = jnp.where(lane == 1, i2.astype(F32), out)
    out = jnp.where(lane == 2, g1, out)
    out = jnp.where(lane == 3, g2, out)
    o_ref[...] = out


def router_top2(x, router, tm=512):
    s, d = x.shape
    lanes = 128
    rp = jnp.zeros((d, lanes), F32).at[:, :N_EXPERTS].set(router)
    return pl.pallas_call(
        _router_kernel,
        grid=(s // tm,),
        in_specs=[pl.BlockSpec((tm, d), lambda m: (m, 0)),
                  pl.BlockSpec((d, lanes), lambda m: (0, 0))],
        out_specs=pl.BlockSpec((tm, lanes), lambda m: (m, 0)),
        out_shape=jax.ShapeDtypeStruct((s, lanes), F32),
        compiler_params=_cparams(("parallel",)),
    )(x, rp)


def _moe_kernel(texp_ref, trows_ref, perm_ref,
                x_hbm, w1_ref, w3_ref, w2_ref, y_hbm, xb, acc, gsem, ssem):
    t = pl.program_id(0)
    f = pl.program_id(1)
    tm = acc.shape[0]
    n_tok = x_hbm.shape[0]
    rows = trows_ref[t]
    base = t * tm

    def gather_copy(i):
        a = perm_ref[base + i]
        tok = jnp.where(a >= n_tok, a - n_tok, a)
        return pltpu.make_async_copy(x_hbm.at[pl.ds(tok, 1), :], acc.at[pl.ds(i, 1), :], gsem)

    def scatter_copy(i):
        return pltpu.make_async_copy(acc.at[pl.ds(i, 1), :], y_hbm.at[pl.ds(perm_ref[base + i], 1), :], ssem)

    def for_rows(fn):
        groups = rows // DMA_UNROLL

        def group_body(gi, carry):
            for u in range(DMA_UNROLL):
                fn(gi * DMA_UNROLL + u)
            return carry

        def row_body(i, carry):
            fn(i)
            return carry

        lax.fori_loop(0, groups, group_body, 0)
        lax.fori_loop(groups * DMA_UNROLL, rows, row_body, 0)

    def start_all(make):
        for_rows(lambda i: make(i).start())

    def wait_all(make):
        for_rows(lambda i: make(i).wait())

    @pl.when(rows > 0)
    def _():
        @pl.when(f == 0)
        def _():
            acc[...] = jnp.zeros_like(acc)
            start_all(gather_copy)
            wait_all(gather_copy)
            xb[...] = acc[...].astype(BF16)
            acc[...] = jnp.zeros_like(acc)

        acc[...] += _swiglu_step(xb[...], w1_ref, w3_ref, w2_ref)

        @pl.when(f == pl.num_programs(1) - 1)
        def _():
            start_all(scatter_copy)
            wait_all(scatter_copy)


def moe_experts(x, w1, w3, w2, layer, tile_expert, tile_rows, perm, tm, tf=256):
    s, d = x.shape
    dff = w1.shape[3]
    n_tiles = tile_expert.shape[0]
    nf = dff // tf
    last = nf - 1

    def fsel(t, f, trows):
        return jnp.where(trows[t] > 0, f, last)

    grid_spec = pltpu.PrefetchScalarGridSpec(
        num_scalar_prefetch=3,
        grid=(n_tiles, nf),
        in_specs=[pl.BlockSpec(memory_space=pl.ANY),
                  pl.BlockSpec((None, None, d, tf), lambda t, f, te, tr, pm: (layer, te[t], 0, fsel(t, f, tr))),
                  pl.BlockSpec((None, None, d, tf), lambda t, f, te, tr, pm: (layer, te[t], 0, fsel(t, f, tr))),
                  pl.BlockSpec((None, None, tf, d), lambda t, f, te, tr, pm: (layer, te[t], fsel(t, f, tr), 0))],
        out_specs=pl.BlockSpec(memory_space=pl.ANY),
        scratch_shapes=[pltpu.VMEM((tm, d), BF16),
                        pltpu.VMEM((tm, d), F32),
                        pltpu.SemaphoreType.DMA(()),
                        pltpu.SemaphoreType.DMA(())],
    )
    return pl.pallas_call(
        _moe_kernel,
        grid_spec=grid_spec,
        out_shape=jax.ShapeDtypeStruct((TOP_K * s, d), F32),
        compiler_params=_cparams(("arbitrary", "arbitrary")),
    )(tile_expert, tile_rows, perm, x, w1, w3, w2)


def _route_plan(e1, e2, tm, n_tiles):
    s = e1.shape[0]
    experts = jnp.concatenate([e1, e2])
    onehot = (experts[:, None] == jnp.arange(N_EXPERTS, dtype=jnp.int32)[None, :]).astype(jnp.int32)
    csum = jnp.cumsum(onehot, axis=0)
    counts = csum[-1]
    tiles_per = (counts + tm - 1) // tm
    tile_end = jnp.cumsum(tiles_per)
    tile_start = tile_end - tiles_per
    pos = jnp.sum(onehot * (tile_start[None, :] * tm + csum - 1), axis=1)
    n_rows = n_tiles * tm
    perm = jnp.zeros((n_rows,), jnp.int32).at[pos].set(jnp.arange(TOP_K * s, dtype=jnp.int32))
    tid = jnp.arange(n_tiles, dtype=jnp.int32)
    texp = jnp.minimum(jnp.sum((tid[:, None] >= tile_end[None, :]).astype(jnp.int32), axis=1), N_EXPERTS - 1)
    used = tid < tile_end[-1]
    local = tid - tile_start[texp]
    trows = jnp.where(used, jnp.clip(counts[texp] - local * tm, 0, tm), 0).astype(jnp.int32)
    last_used = texp[jnp.maximum(tile_end[-1] - 1, 0)]
    texp = jnp.where(used, texp, last_used).astype(jnp.int32)
    return texp, trows, perm


def _combine_kernel(h_ref, y1_ref, y2_ref, gt_ref, g_ref, *out_refs, want_h):
    gates = gt_ref[...]
    hnew = h_ref[...] + gates[:, 2:3] * y1_ref[...] + gates[:, 3:4] * y2_ref[...]
    if want_h:
        out_refs[0][...] = hnew
    out_refs[-1][...] = _rms(hnew, g_ref[...]).astype(out_refs[-1].dtype)


def combine_residual_norm(h, y, route, g, want_h, norm_dtype, tm=512):
    s, d = h.shape
    nb = s // tm
    row = lambda m: (m, 0)
    outs_spec = [pl.BlockSpec((tm, d), row)]
    outs_shape = [jax.ShapeDtypeStruct((s, d), norm_dtype)]
    if want_h:
        outs_spec = [pl.BlockSpec((tm, d), row)] + outs_spec
        outs_shape = [jax.ShapeDtypeStruct((s, d), F32)] + outs_shape
    return pl.pallas_call(
        functools.partial(_combine_kernel, want_h=want_h),
        grid=(nb,),
        in_specs=[pl.BlockSpec((tm, d), row),
                  pl.BlockSpec((tm, d), row),
                  pl.BlockSpec((tm, d), lambda m: (m + nb, 0)),
                  pl.BlockSpec((tm, route.shape[1]), row),
                  pl.BlockSpec((1, d), lambda m: (0, 0))],
        out_specs=outs_spec,
        out_shape=outs_shape,
        compiler_params=_cparams(("parallel",)),
    )(h, y, y, route, g.reshape(1, d))


MOE_TM = 1088


def kernel(x, even_norm1, even_w_in, even_conv_w, even_conv_b, even_cln_g, even_cln_b, even_w_out, even_norm2, even_ffn_w1, even_ffn_w3, even_ffn_w2, odd_norm1, odd_ssm_w_in, odd_lam_re, odd_lam_im, odd_log_dt, odd_b_re, odd_b_im, odd_c_re, odd_c_im, odd_d_skip, odd_glu_wa, odd_glu_wg, odd_norm2, odd_router, odd_moe_w1, odd_moe_w3, odd_moe_w2, final_norm):
    b, s, d = x.shape
    assert b == 1
    depth = even_norm1.shape[0] + odd_norm1.shape[0]
    assert even_norm1.shape[0] == odd_norm1.shape[0]
    h = x.reshape(s, d)
    hn = rmsnorm(h, even_norm1[0], BF16)
    n_tiles = (TOP_K * s) // MOE_TM + N_EXPERTS
    out = None
    ffn_out = None
    for layer in range(depth):
        j = layer // 2
        if layer % 2 == 0:
            proj = matmul(hn, even_w_in, j)
            att = dilated_attention(proj)
            cv = conv_module(proj, even_conv_w[j], even_conv_b[j], even_cln_g[j], even_cln_b[j])
            h, hn = outproj_residual_norm(att, cv, even_w_out, j, h, even_norm2[j])
            ffn_out = ffn(hn, even_ffn_w1, even_ffn_w3, even_ffn_w2, j)
        else:
            h, u = residual_norm_matmul(h, ffn_out, odd_norm1[j], odd_ssm_w_in, j)
            y = ssm_core(u, (odd_lam_re[j], odd_lam_im[j], odd_log_dt[j], odd_b_re[j], odd_b_im[j],
                             odd_c_re[j], odd_c_im[j], odd_d_skip[j]))
            h, hn = gluproj_residual_norm(y, odd_glu_wa, odd_glu_wg, j, h, odd_norm2[j])
            route = router_top2(hn, odd_router[j])
            e1 = route[:, 0].astype(jnp.int32)
            e2 = route[:, 1].astype(jnp.int32)
            texp, trows, perm = _route_plan(e1, e2, MOE_TM, n_tiles)
            ys = moe_experts(hn, odd_moe_w1, odd_moe_w3, odd_moe_w2, j, texp, trows, perm, MOE_TM)
            if layer + 1 < depth:
                h, hn = combine_residual_norm(h, ys, route, even_norm1[j + 1], True, BF16)
            else:
                (out,) = combine_residual_norm(h, ys, route, final_norm, False, F32)
    return out.reshape(b, s, d)
```

```python
import functools
import math

import jax
import jax.numpy as jnp
from jax import lax
from jax.experimental import pallas as pl
from jax.experimental.pallas import tpu as pltpu

F32 = jnp.float32
BF16 = jnp.bfloat16

NORM_EPS = 1e-5
N_HEADS = 8
HEAD_DIM = 128
D_ATT = N_HEADS * HEAD_DIM
DILATIONS = (1, 4, 16)
BAND = 128
ATT_CHUNK = BAND * DILATIONS[-1]
ATT_UNROLL = 4
ALIBI_MAX_BIAS = 8.0
D_CONV = 1024
CONV_WIDTH = 31
CONV_HALO = 32
SSM_GROUP = 16
SSM_STATE = 64
SSM_CHUNK = 16
SSM_LANES = 128
N_EXPERTS = 8
TOP_K = 2
DMA_UNROLL = 8
MOE_ROW_STEP = 128

VMEM_LIMIT = 56 * 1024 * 1024


def _cparams(sem):
    return pltpu.CompilerParams(dimension_semantics=sem, vmem_limit_bytes=VMEM_LIMIT)


def _rms(x, g):
    return x * lax.rsqrt(jnp.mean(x * x, axis=-1, keepdims=True) + NORM_EPS) * g


def _rmsnorm_kernel(x_ref, g_ref, o_ref):
    o_ref[...] = _rms(x_ref[...], g_ref[...]).astype(o_ref.dtype)


def rmsnorm(x, g, out_dtype, tm=512):
    s, d = x.shape
    return pl.pallas_call(
        _rmsnorm_kernel,
        grid=(s // tm,),
        in_specs=[pl.BlockSpec((tm, d), lambda m: (m, 0)),
                  pl.BlockSpec((1, d), lambda m: (0, 0))],
        out_specs=pl.BlockSpec((tm, d), lambda m: (m, 0)),
        out_shape=jax.ShapeDtypeStruct((s, d), out_dtype),
        compiler_params=_cparams(("parallel",)),
    )(x, g.reshape(1, d))


def _matmul_kernel(x_ref, w_ref, o_ref, wb_ref):
    @pl.when(pl.program_id(1) == 0)
    def _():
        wb_ref[...] = w_ref[...].astype(BF16)

    o_ref[...] = jnp.dot(x_ref[...], wb_ref[...], preferred_element_type=F32).astype(o_ref.dtype)


def matmul(x, w, layer, out_dtype=F32, tm=512, tn=1024):
    s, k = x.shape
    n = w.shape[2]
    tn = min(tn, n)
    return pl.pallas_call(
        _matmul_kernel,
        grid=(n // tn, s // tm),
        in_specs=[pl.BlockSpec((tm, k), lambda j, m: (m, 0)),
                  pl.BlockSpec((None, k, tn), lambda j, m: (layer, 0, j))],
        out_specs=pl.BlockSpec((tm, tn), lambda j, m: (m, j)),
        out_shape=jax.ShapeDtypeStruct((s, n), out_dtype),
        scratch_shapes=[pltpu.VMEM((k, tn), BF16)],
        compiler_params=_cparams(("arbitrary", "arbitrary")),
    )(x, w)


def _resnorm_matmul_kernel(h_ref, y_ref, g_ref, w_ref, hnew_ref, o_ref, wb_ref):
    @pl.when(pl.program_id(0) == 0)
    def _():
        wb_ref[...] = w_ref[...].astype(BF16)

    hnew = h_ref[...] + y_ref[...]
    hnew_ref[...] = hnew
    xn = _rms(hnew, g_ref[...]).astype(BF16)
    o_ref[...] = jnp.dot(xn, wb_ref[...], preferred_element_type=F32).astype(o_ref.dtype)


def residual_norm_matmul(h, y, g, w, layer, tm=256):
    s, d = h.shape
    n = w.shape[2]
    row = lambda m: (m, 0)
    return pl.pallas_call(
        _resnorm_matmul_kernel,
        grid=(s // tm,),
        in_specs=[pl.BlockSpec((tm, d), row),
                  pl.BlockSpec((tm, d), row),
                  pl.BlockSpec((1, d), lambda m: (0, 0)),
                  pl.BlockSpec((None, d, n), lambda m: (layer, 0, 0))],
        out_specs=[pl.BlockSpec((tm, d), row), pl.BlockSpec((tm, n), row)],
        out_shape=[jax.ShapeDtypeStruct((s, d), F32), jax.ShapeDtypeStruct((s, n), F32)],
        scratch_shapes=[pltpu.VMEM((d, n), BF16)],
        compiler_params=_cparams(("arbitrary",)),
    )(h, y, g.reshape(1, d), w)


def _attn_kernel(q_ref, kp_ref, kc_ref, vp_ref, vc_ref, o_ref, kk, vv, osc, lsc):
    h = pl.program_id(0)
    c = pl.program_id(1)
    kk[0:ATT_CHUNK, :] = kp_ref[...]
    kk[ATT_CHUNK:2 * ATT_CHUNK, :] = kc_ref[...]
    vv[0:ATT_CHUNK, :] = vp_ref[...]
    vv[ATT_CHUNK:2 * ATT_CHUNK, :] = vc_ref[...]

    qi = lax.broadcasted_iota(jnp.int32, (BAND, 2 * BAND), 0)
    kj = lax.broadcasted_iota(jnp.int32, (BAND, 2 * BAND), 1)
    dist = BAND + qi - kj
    band = (dist >= 0) & (dist <= BAND)
    head = jnp.full((BAND, 2 * BAND), h + 1, jnp.int32).astype(F32)
    slope = jnp.exp2(-(ALIBI_MAX_BIAS / N_HEADS) * head)
    sdist = slope * dist.astype(F32)
    scale = HEAD_DIM ** -0.5
    blocks = ATT_CHUNK // BAND

    for g, d in enumerate(DILATIONS):
        bias = jnp.where(band, -sdist * float(d), -jnp.inf)
        bias_first = jnp.where(kj >= BAND, bias, -jnp.inf)

        def one_block(b, g=g, d=d, bias=bias, bias_first=bias_first):
            n = b // d
            r = b % d
            qstart = n * (BAND * d) + r
            kstart = ATT_CHUNK + (n - 1) * (BAND * d) + r
            q = q_ref[pl.ds(qstart, BAND, stride=d), :].astype(BF16)
            k = kk[pl.ds(kstart, 2 * BAND, stride=d), :].astype(BF16)
            v = vv[pl.ds(kstart, 2 * BAND, stride=d), :].astype(BF16)
            s = lax.dot_general(q, k, (((1,), (1,)), ((), ())), preferred_element_type=F32) * scale
            s = s + jnp.where((c == 0) & (n == 0), bias_first, bias)
            m = jnp.max(s, axis=-1, keepdims=True)
            p = jnp.exp(s - m)
            den = jnp.sum(p, axis=-1, keepdims=True)
            o = jnp.dot(p.astype(BF16), v, preferred_element_type=F32) / den
            lse = m + jnp.log(den)
            osc[g, pl.ds(qstart, BAND, stride=d), :] = o
            lsc[g, pl.ds(qstart, BAND, stride=d), :] = jnp.broadcast_to(lse, (BAND, HEAD_DIM))

        def body(i, carry, one_block=one_block):
            for u in range(ATT_UNROLL):
                one_block(i * ATT_UNROLL + u)
            return carry

        lax.fori_loop(0, blocks // ATT_UNROLL, body, 0)

    l0, l1, l2 = lsc[0], lsc[1], lsc[2]
    m = jnp.maximum(jnp.maximum(l0, l1), l2)
    w0, w1, w2 = jnp.exp(l0 - m), jnp.exp(l1 - m), jnp.exp(l2 - m)
    att = (w0 * osc[0] + w1 * osc[1] + w2 * osc[2]) / (w0 + w1 + w2)
    o_ref[...] = att.astype(o_ref.dtype)


def dilated_attention(proj):
    s = proj.shape[0]
    blk = (ATT_CHUNK, HEAD_DIM)
    prev = lambda c: jnp.maximum(c - 1, 0)
    return pl.pallas_call(
        _attn_kernel,
        grid=(N_HEADS, s // ATT_CHUNK),
        in_specs=[pl.BlockSpec(blk, lambda h, c: (c, h)),
                  pl.BlockSpec(blk, lambda h, c: (prev(c), N_HEADS + h)),
                  pl.BlockSpec(blk, lambda h, c: (c, N_HEADS + h)),
                  pl.BlockSpec(blk, lambda h, c: (prev(c), 2 * N_HEADS + h)),
                  pl.BlockSpec(blk, lambda h, c: (c, 2 * N_HEADS + h))],
        out_specs=pl.BlockSpec(blk, lambda h, c: (c, h)),
        out_shape=jax.ShapeDtypeStruct((s, D_ATT), BF16),
        scratch_shapes=[pltpu.VMEM((2 * ATT_CHUNK, HEAD_DIM), F32),
                        pltpu.VMEM((2 * ATT_CHUNK, HEAD_DIM), F32),
                        pltpu.VMEM((len(DILATIONS), ATT_CHUNK, HEAD_DIM), F32),
                        pltpu.VMEM((len(DILATIONS), ATT_CHUNK, HEAD_DIM), F32)],
        compiler_params=_cparams(("parallel", "parallel")),
    )(proj, proj, proj, proj, proj)


def _conv_kernel(val_ref, gate_ref, hval_ref, hgate_ref, w_ref, b_ref, lg_ref, lb_ref, o_ref, glu):
    m = pl.program_id(0)
    tm = val_ref.shape[0]
    halo = hval_ref[...] * jax.nn.sigmoid(hgate_ref[...])
    glu[0:CONV_HALO, :] = jnp.where(m > 0, halo, 0.0)
    glu[CONV_HALO:CONV_HALO + tm, :] = val_ref[...] * jax.nn.sigmoid(gate_ref[...])
    first = CONV_HALO - (CONV_WIDTH - 1)

    acc = jnp.broadcast_to(b_ref[...], (tm, D_CONV))
    for j in range(CONV_WIDTH):
        acc = acc + w_ref[j:j + 1, :] * glu[pl.ds(first + j, tm), :]
    mu = jnp.mean(acc, axis=-1, keepdims=True)
    cen = acc - mu
    var = jnp.mean(cen * cen, axis=-1, keepdims=True)
    y = cen * lax.rsqrt(var + NORM_EPS) * lg_ref[...] + lb_ref[...]
    o_ref[...] = (y * jax.nn.sigmoid(y)).astype(o_ref.dtype)


def conv_module(proj, conv_w, conv_b, ln_g, ln_b, tm=512):
    s = proj.shape[0]
    vcol = 3 * D_ATT // D_CONV
    gcol = vcol + 1
    per = tm // CONV_HALO
    hrow = lambda m: jnp.maximum(m * per - 1, 0)
    vec = pl.BlockSpec((1, D_CONV), lambda m: (0, 0))
    return pl.pallas_call(
        _conv_kernel,
        grid=(s // tm,),
        in_specs=[pl.BlockSpec((tm, D_CONV), lambda m: (m, vcol)),
                  pl.BlockSpec((tm, D_CONV), lambda m: (m, gcol)),
                  pl.BlockSpec((CONV_HALO, D_CONV), lambda m: (hrow(m), vcol)),
                  pl.BlockSpec((CONV_HALO, D_CONV), lambda m: (hrow(m), gcol)),
                  pl.BlockSpec((CONV_WIDTH, D_CONV), lambda m: (0, 0)),
                  vec, vec, vec],
        out_specs=pl.BlockSpec((tm, D_CONV), lambda m: (m, 0)),
        out_shape=jax.ShapeDtypeStruct((s, D_CONV), BF16),
        scratch_shapes=[pltpu.VMEM((CONV_HALO + tm, D_CONV), F32)],
        compiler_params=_cparams(("parallel",)),
    )(proj, proj, proj, proj, conv_w, conv_b.reshape(1, -1), ln_g.reshape(1, -1), ln_b.reshape(1, -1))


def _outproj_kernel(att_ref, cv_ref, w_ref, h_ref, g_ref, hnew_ref, hn_ref, wb_ref):
    @pl.when(pl.program_id(0) == 0)
    def _():
        wb_ref[...] = w_ref[...].astype(BF16)

    y = jnp.dot(att_ref[...], wb_ref[0:D_ATT, :], preferred_element_type=F32)
    y = y + jnp.dot(cv_ref[...], wb_ref[D_ATT:D_ATT + D_CONV, :], preferred_element_type=F32)
    hnew = h_ref[...] + y
    hnew_ref[...] = hnew
    hn_ref[...] = _rms(hnew, g_ref[...]).astype(hn_ref.dtype)


def outproj_residual_norm(att, cv, w_out, layer, h, g, tm=512):
    s, d = h.shape
    kin = w_out.shape[1]
    row = lambda m: (m, 0)
    fixed = lambda m: (0, 0)
    return pl.pallas_call(
        _outproj_kernel,
        grid=(s // tm,),
        in_specs=[pl.BlockSpec((tm, D_ATT), row),
                  pl.BlockSpec((tm, D_CONV), row),
                  pl.BlockSpec((None, kin, d), lambda m: (layer, 0, 0), pipeline_mode=pl.Buffered(1)),
                  pl.BlockSpec((tm, d), row),
                  pl.BlockSpec((1, d), fixed)],
        out_specs=[pl.BlockSpec((tm, d), row), pl.BlockSpec((tm, d), row)],
        out_shape=[jax.ShapeDtypeStruct((s, d), F32), jax.ShapeDtypeStruct((s, d), BF16)],
        scratch_shapes=[pltpu.VMEM((kin, d), BF16)],
        compiler_params=_cparams(("arbitrary",)),
    )(att, cv, w_out, h, g.reshape(1, d))


def _gluproj_kernel(y_ref, wa_ref, wg_ref, h_ref, g_ref, hnew_ref, hn_ref, wab_ref, wgb_ref):
    @pl.when(pl.program_id(0) == 0)
    def _():
        wab_ref[...] = wa_ref[...].astype(BF16)
        wgb_ref[...] = wg_ref[...].astype(BF16)

    y = y_ref[...].astype(BF16)
    a = jnp.dot(y, wab_ref[...], preferred_element_type=F32)
    gt = jnp.dot(y, wgb_ref[...], preferred_element_type=F32)
    hnew = h_ref[...] + a * jax.nn.sigmoid(gt)
    hnew_ref[...] = hnew
    hn_ref[...] = _rms(hnew, g_ref[...]).astype(hn_ref.dtype)


def gluproj_residual_norm(y, wa, wg, layer, h, g, tm=256):
    s, d = h.shape
    kin = wa.shape[1]
    row = lambda m: (m, 0)
    fixed = lambda m: (0, 0)
    once = lambda m: (layer, 0, 0)
    return pl.pallas_call(
        _gluproj_kernel,
        grid=(s // tm,),
        in_specs=[pl.BlockSpec((tm, kin), row),
                  pl.BlockSpec((None, kin, d), once, pipeline_mode=pl.Buffered(1)),
                  pl.BlockSpec((None, kin, d), once, pipeline_mode=pl.Buffered(1)),
                  pl.BlockSpec((tm, d), row),
                  pl.BlockSpec((1, d), fixed)],
        out_specs=[pl.BlockSpec((tm, d), row), pl.BlockSpec((tm, d), row)],
        out_shape=[jax.ShapeDtypeStruct((s, d), F32), jax.ShapeDtypeStruct((s, d), F32)],
        scratch_shapes=[pltpu.VMEM((kin, d), BF16), pltpu.VMEM((kin, d), BF16)],
        compiler_params=_cparams(("arbitrary",)),
    )(y, wa, wg, h, g.reshape(1, d))


def _swiglu_step(xb, w1_ref, w3_ref, w2_ref):
    a = jnp.dot(xb, w1_ref[...].astype(BF16), preferred_element_type=F32)
    b = jnp.dot(xb, w3_ref[...].astype(BF16), preferred_element_type=F32)
    p = (a * jax.nn.sigmoid(a) * b).astype(BF16)
    return jnp.dot(p, w2_ref[...].astype(BF16), preferred_element_type=F32)


def _ffn_kernel(x_ref, w1_ref, w3_ref, w2_ref, y_ref):
    @pl.when(pl.program_id(1) == 0)
    def _():
        y_ref[...] = jnp.zeros_like(y_ref)

    y_ref[...] += _swiglu_step(x_ref[...], w1_ref, w3_ref, w2_ref)


def ffn(x, w1, w3, w2, layer, tm=1024, tf=256):
    s, d = x.shape
    dff = w1.shape[2]
    row = lambda m, f: (m, 0)
    return pl.pallas_call(
        _ffn_kernel,
        grid=(s // tm, dff // tf),
        in_specs=[pl.BlockSpec((tm, d), row),
                  pl.BlockSpec((None, d, tf), lambda m, f: (layer, 0, f)),
                  pl.BlockSpec((None, d, tf), lambda m, f: (layer, 0, f)),
                  pl.BlockSpec((None, tf, d), lambda m, f: (layer, f, 0))],
        out_specs=pl.BlockSpec((tm, d), row),
        out_shape=jax.ShapeDtypeStruct((s, d), F32),
        compiler_params=_cparams(("parallel", "arbitrary")),
    )(x, w1, w3, w2)


def _ssm_tables(lam_re, lam_im, log_dt, b_re, b_im, c_re, c_im, d_skip, n_doublings):
    L = SSM_CHUNK
    G, P = lam_re.shape
    dt = jnp.exp(log_dt)[:, None]
    mag = jnp.exp(lam_re * dt)
    a_re = mag * jnp.cos(lam_im * dt)
    a_im = mag * jnp.sin(lam_im * dt)
    inv = 1.0 / (lam_re * lam_re + lam_im * lam_im)
    f_re = ((a_re - 1.0) * lam_re + a_im * lam_im) * inv
    f_im = (a_im * lam_re - (a_re - 1.0) * lam_im) * inv
    bb_re = f_re[..., None] * b_re - f_im[..., None] * b_im
    bb_im = f_re[..., None] * b_im + f_im[..., None] * b_re

    pw_re = [jnp.ones_like(a_re)]
    pw_im = [jnp.zeros_like(a_im)]
    for _ in range(L):
        pr, pi = pw_re[-1], pw_im[-1]
        pw_re.append(pr * a_re - pi * a_im)
        pw_im.append(pr * a_im + pi * a_re)
    pw_re = jnp.stack(pw_re)
    pw_im = jnp.stack(pw_im)

    ab_re = pw_re[:L, :, :, None] * bb_re[None] - pw_im[:L, :, :, None] * bb_im[None]
    ab_im = pw_re[:L, :, :, None] * bb_im[None] + pw_im[:L, :, :, None] * bb_re[None]
    hi = lax.Precision.HIGHEST
    kern = (jnp.einsum('gcp,tgpd->tgcd', c_re, ab_re, precision=hi)
            - jnp.einsum('gcp,tgpd->tgcd', c_im, ab_im, precision=hi))
    C = SSM_GROUP
    Q = SSM_LANES // C
    O = G // Q
    kern = kern.at[0].add(jnp.eye(C, dtype=F32)[None] * d_skip.reshape(G, C)[:, :, None])

    kc = kern.reshape(L, O, Q, C, C).transpose(1, 4, 0, 2, 3).reshape(O, C, L * Q * C)

    def by_state(t):
        return jnp.stack(t, axis=3).reshape(O, L, C, 2 * Q * P)

    bc = by_state([ab[::-1].reshape(L, O, Q, P, C).transpose(1, 0, 4, 2, 3) for ab in (ab_re, ab_im)])

    ca_re = c_re[None] * pw_re[1:, :, None, :] - c_im[None] * pw_im[1:, :, None, :]
    ca_im = c_re[None] * pw_im[1:, :, None, :] + c_im[None] * pw_re[1:, :, None, :]
    cc = by_state([ca.reshape(L, O, Q, C, P).transpose(1, 0, 3, 2, 4) for ca in (ca_re, -ca_im)])

    sq_re, sq_im = pw_re[L].reshape(O, Q * P), pw_im[L].reshape(O, Q * P)
    same, cross = [], []
    for _ in range(n_doublings):
        same.append(jnp.concatenate([sq_re, sq_re], axis=-1))
        cross.append(jnp.concatenate([-sq_im, sq_im], axis=-1))
        sq_re, sq_im = sq_re * sq_re - sq_im * sq_im, 2.0 * sq_re * sq_im
    return kc, bc, cc, jnp.stack(same, axis=1), jnp.stack(cross, axis=1)


def _ssm_kernel(u_ref, kc_ref, bc_ref, cc_ref, same_ref, cross_ref, y_ref, yacc, kcat):
    L = SSM_CHUNK
    nc = yacc.shape[0]
    n_state = same_ref.shape[1]
    half = n_state // 2
    groups = SSM_LANES // SSM_GROUP

    def spread(table, col_group):
        rows = lax.broadcasted_iota(jnp.int32, (SSM_LANES, table.shape[1]), 0)
        cols = lax.broadcasted_iota(jnp.int32, (SSM_LANES, table.shape[1]), 1)
        tiled = jnp.concatenate([table] * groups, axis=0)
        return jnp.where(rows // SSM_GROUP == col_group(cols), tiled, 0.0).astype(BF16)

    state_group = lambda cols: (cols % half) // SSM_STATE
    lane_group = lambda cols: (cols % SSM_LANES) // SSM_GROUP

    kcat[...] = spread(kc_ref[...], lane_group)
    yacc[...] = jnp.zeros_like(yacc)
    x = jnp.zeros((nc, n_state), F32)
    for j in range(L):
        uj = u_ref[pl.ds(j, nc, stride=L), :].astype(BF16)
        x = x + jnp.dot(uj, spread(bc_ref[j], state_group), preferred_element_type=F32)
        width = (L - j) * SSM_LANES
        yacc[:, j * SSM_LANES:] += jnp.dot(uj, kcat[:, :width], preferred_element_type=F32)
    row = lax.broadcasted_iota(jnp.int32, x.shape, 0)
    for k in range(same_ref.shape[0]):
        shift = 1 << k
        sh = jnp.where(row >= shift, pltpu.roll(x, shift, 0), 0.0)
        x = x + same_ref[k:k + 1, :] * sh + cross_ref[k:k + 1, :] * pltpu.roll(sh, half, 1)
    prev = jnp.where(row >= 1, pltpu.roll(x, 1, 0), 0.0).astype(BF16)
    for s in range(L):
        carried = lax.dot_general(prev, spread(cc_ref[s], state_group), (((1,), (1,)), ((), ())),
                                  preferred_element_type=F32)
        y_ref[pl.ds(s, nc, stride=L), :] = jax.nn.gelu(yacc[:, s * SSM_LANES:(s + 1) * SSM_LANES] + carried)


def ssm_core(u, params):
    s, width = u.shape
    L = SSM_CHUNK
    nc = s // L
    n_doublings = max(1, (nc - 1).bit_length())
    kc, bc, cc, same, cross = _ssm_tables(*params, n_doublings=n_doublings)
    n_blocks = width // SSM_LANES
    n_state = same.shape[2]
    blk3 = lambda o: (o, 0, 0)
    blk4 = lambda o: (o, 0, 0, 0)
    return pl.pallas_call(
        _ssm_kernel,
        grid=(n_blocks,),
        in_specs=[pl.BlockSpec((s, SSM_LANES), lambda o: (0, o)),
                  pl.BlockSpec((None, SSM_GROUP, L * SSM_LANES), blk3),
                  pl.BlockSpec((None, L, SSM_GROUP, n_state), blk4),
                  pl.BlockSpec((None, L, SSM_GROUP, n_state), blk4),
                  pl.BlockSpec((None, n_doublings, n_state), blk3),
                  pl.BlockSpec((None, n_doublings, n_state), blk3)],
        out_specs=pl.BlockSpec((s, SSM_LANES), lambda o: (0, o)),
        out_shape=jax.ShapeDtypeStruct((s, width), F32),
        scratch_shapes=[pltpu.VMEM((nc, L * SSM_LANES), F32),
                        pltpu.VMEM((SSM_LANES, L * SSM_LANES), BF16)],
        compiler_params=_cparams(("parallel",)),
    )(u, kc, bc, cc, same, cross)


def _router_kernel(x_ref, r_ref, o_ref):
    logits = jnp.dot(x_ref[...], r_ref[...], preferred_element_type=F32, precision=lax.Precision.HIGHEST)
    lane = lax.broadcasted_iota(jnp.int32, logits.shape, 1)
    logits = jnp.where(lane < N_EXPERTS, logits, -jnp.inf)
    big = jnp.int32(logits.shape[1])
    v1 = jnp.max(logits, axis=-1, keepdims=True)
    i1 = jnp.min(jnp.where(logits == v1, lane, big), axis=-1, keepdims=True)
    rest = jnp.where(lane == i1, -jnp.inf, logits)
    v2 = jnp.max(rest, axis=-1, keepdims=True)
    i2 = jnp.min(jnp.where(rest == v2, lane, big), axis=-1, keepdims=True)
    e2 = jnp.exp(v2 - v1)
    g1 = 1.0 / (1.0 + e2)
    g2 = e2 / (1.0 + e2)
    out = jnp.where(lane == 0, i1.astype(F32), 0.0)
    out = jnp.where(lane == 1, i2.astype(F32), out)
    out = jnp.where(lane == 2, g1, out)
    out = jnp.where(lane == 3, g2, out)
    o_ref[...] = out


def router_top2(x, router, tm=512):
    s, d = x.shape
    lanes = 128
    rp = jnp.zeros((d, lanes), F32).at[:, :N_EXPERTS].set(router)
    return pl.pallas_call(
        _router_kernel,
        grid=(s // tm,),
        in_specs=[pl.BlockSpec((tm, d), lambda m: (m, 0)),
                  pl.BlockSpec((d, lanes), lambda m: (0, 0))],
        out_specs=pl.BlockSpec((tm, lanes), lambda m: (m, 0)),
        out_shape=jax.ShapeDtypeStruct((s, lanes), F32),
        compiler_params=_cparams(("parallel",)),
    )(x, rp)


def _moe_kernel(texp_ref, trows_ref, perm_ref,
                x_hbm, w1_ref, w3_ref, w2_ref, y_hbm, xb, acc, gsem, ssem):
    t = pl.program_id(0)
    f = pl.program_id(1)
    tm = acc.shape[0]
    n_tok = x_hbm.shape[0]
    rows = trows_ref[t]
    base = t * tm

    def gather_copy(i):
        a = perm_ref[base + i]
        tok = jnp.where(a >= n_tok, a - n_tok, a)
        return pltpu.make_async_copy(x_hbm.at[pl.ds(tok, 1), :], acc.at[pl.ds(i, 1), :], gsem)

    def scatter_copy(i):
        return pltpu.make_async_copy(acc.at[pl.ds(i, 1), :], y_hbm.at[pl.ds(perm_ref[base + i], 1), :], ssem)

    def for_rows(fn):
        groups = rows // DMA_UNROLL

        def group_body(gi, carry):
            for u in range(DMA_UNROLL):
                fn(gi * DMA_UNROLL + u)
            return carry

        def row_body(i, carry):
            fn(i)
            return carry

        lax.fori_loop(0, groups, group_body, 0)
        lax.fori_loop(groups * DMA_UNROLL, rows, row_body, 0)

    def start_all(make):
        for_rows(lambda i: make(i).start())

    def wait_all(make):
        for_rows(lambda i: make(i).wait())

    @pl.when(rows > 0)
    def _():
        @pl.when(f == 0)
        def _():
            acc[...] = jnp.zeros_like(acc)
            start_all(gather_copy)
            wait_all(gather_copy)
            xb[...] = acc[...].astype(BF16)
            acc[...] = jnp.zeros_like(acc)

        lo = 0
        for m in _moe_row_buckets(tm):
            @pl.when((rows > lo) & (rows <= m))
            def _(m=m):
                acc[0:m, :] += _swiglu_step(xb[0:m, :], w1_ref, w3_ref, w2_ref)
            lo = m

        @pl.when(f == pl.num_programs(1) - 1)
        def _():
            start_all(scatter_copy)
            wait_all(scatter_copy)


def _moe_row_buckets(tm):
    return tuple(range(2 * MOE_ROW_STEP, tm, MOE_ROW_STEP)) + (tm,)


def moe_experts(x, w1, w3, w2, layer, tile_expert, tile_rows, perm, tm, tf=256):
    s, d = x.shape
    dff = w1.shape[3]
    n_tiles = tile_expert.shape[0]
    nf = dff // tf
    last = nf - 1

    def fsel(t, f, trows):
        return jnp.where(trows[t] > 0, f, last)

    grid_spec = pltpu.PrefetchScalarGridSpec(
        num_scalar_prefetch=3,
        grid=(n_tiles, nf),
        in_specs=[pl.BlockSpec(memory_space=pl.ANY),
                  pl.BlockSpec((None, None, d, tf), lambda t, f, te, tr, pm: (layer, te[t], 0, fsel(t, f, tr))),
                  pl.BlockSpec((None, None, d, tf), lambda t, f, te, tr, pm: (layer, te[t], 0, fsel(t, f, tr))),
                  pl.BlockSpec((None, None, tf, d), lambda t, f, te, tr, pm: (layer, te[t], fsel(t, f, tr), 0))],
        out_specs=pl.BlockSpec(memory_space=pl.ANY),
        scratch_shapes=[pltpu.VMEM((tm, d), BF16),
                        pltpu.VMEM((tm, d), F32),
                        pltpu.SemaphoreType.DMA(()),
                        pltpu.SemaphoreType.DMA(())],
    )
    return pl.pallas_call(
        _moe_kernel,
        grid_spec=grid_spec,
        out_shape=jax.ShapeDtypeStruct((TOP_K * s, d), F32),
        compiler_params=_cparams(("arbitrary", "arbitrary")),
    )(tile_expert, tile_rows, perm, x, w1, w3, w2)


def _route_plan(e1, e2, tm, n_tiles):
    s = e1.shape[0]
    experts = jnp.concatenate([e1, e2])
    onehot = (experts[:, None] == jnp.arange(N_EXPERTS, dtype=jnp.int32)[None, :]).astype(jnp.int32)
    csum = jnp.cumsum(onehot, axis=0)
    counts = csum[-1]
    tiles_per = (counts + tm - 1) // tm
    tile_end = jnp.cumsum(tiles_per)
    tile_start = tile_end - tiles_per
    per_tile = (counts + jnp.maximum(tiles_per, 1) - 1) // jnp.maximum(tiles_per, 1)
    rank = jnp.sum(onehot * (csum - 1), axis=1)
    own_start = jnp.sum(onehot * tile_start[None, :], axis=1)
    own_per_tile = jnp.maximum(jnp.sum(onehot * per_tile[None, :], axis=1), 1)
    pos = (own_start + rank // own_per_tile) * tm + rank % own_per_tile
    n_rows = n_tiles * tm
    perm = jnp.zeros((n_rows,), jnp.int32).at[pos].set(jnp.arange(TOP_K * s, dtype=jnp.int32))
    tid = jnp.arange(n_tiles, dtype=jnp.int32)
    texp = jnp.minimum(jnp.sum((tid[:, None] >= tile_end[None, :]).astype(jnp.int32), axis=1), N_EXPERTS - 1)
    used = tid < tile_end[-1]
    local = tid - tile_start[texp]
    trows = jnp.where(used, jnp.clip(counts[texp] - local * per_tile[texp], 0, per_tile[texp]), 0).astype(jnp.int32)
    last_used = texp[jnp.maximum(tile_end[-1] - 1, 0)]
    texp = jnp.where(used, texp, last_used).astype(jnp.int32)
    return texp, trows, perm


def _combine_kernel(h_ref, y1_ref, y2_ref, gt_ref, g_ref, *out_refs, want_h):
    gates = gt_ref[...]
    hnew = h_ref[...] + gates[:, 2:3] * y1_ref[...] + gates[:, 3:4] * y2_ref[...]
    if want_h:
        out_refs[0][...] = hnew
    out_refs[-1][...] = _rms(hnew, g_ref[...]).astype(out_refs[-1].dtype)


def combine_residual_norm(h, y, route, g, want_h, norm_dtype, tm=512):
    s, d = h.shape
    nb = s // tm
    row = lambda m: (m, 0)
    outs_spec = [pl.BlockSpec((tm, d), row)]
    outs_shape = [jax.ShapeDtypeStruct((s, d), norm_dtype)]
    if want_h:
        outs_spec = [pl.BlockSpec((tm, d), row)] + outs_spec
        outs_shape = [jax.ShapeDtypeStruct((s, d), F32)] + outs_shape
    return pl.pallas_call(
        functools.partial(_combine_kernel, want_h=want_h),
        grid=(nb,),
        in_specs=[pl.BlockSpec((tm, d), row),
                  pl.BlockSpec((tm, d), row),
                  pl.BlockSpec((tm, d), lambda m: (m + nb, 0)),
                  pl.BlockSpec((tm, route.shape[1]), row),
                  pl.BlockSpec((1, d), lambda m: (0, 0))],
        out_specs=outs_spec,
        out_shape=outs_shape,
        compiler_params=_cparams(("parallel",)),
    )(h, y, y, route, g.reshape(1, d))


MOE_TM = 1088


def kernel(x, even_norm1, even_w_in, even_conv_w, even_conv_b, even_cln_g, even_cln_b, even_w_out, even_norm2, even_ffn_w1, even_ffn_w3, even_ffn_w2, odd_norm1, odd_ssm_w_in, odd_lam_re, odd_lam_im, odd_log_dt, odd_b_re, odd_b_im, odd_c_re, odd_c_im, odd_d_skip, odd_glu_wa, odd_glu_wg, odd_norm2, odd_router, odd_moe_w1, odd_moe_w3, odd_moe_w2, final_norm):
    b, s, d = x.shape
    assert b == 1
    depth = even_norm1.shape[0] + odd_norm1.shape[0]
    assert even_norm1.shape[0] == odd_norm1.shape[0]
    h = x.reshape(s, d)
    hn = rmsnorm(h, even_norm1[0], BF16)
    n_tiles = (TOP_K * s) // MOE_TM + N_EXPERTS
    out = None
    ffn_out = None
    for layer in range(depth):
        j = layer // 2
        if layer % 2 == 0:
            proj = matmul(hn, even_w_in, j)
            att = dilated_attention(proj)
            cv = conv_module(proj, even_conv_w[j], even_conv_b[j], even_cln_g[j], even_cln_b[j])
            h, hn = outproj_residual_norm(att, cv, even_w_out, j, h, even_norm2[j])
            ffn_out = ffn(hn, even_ffn_w1, even_ffn_w3, even_ffn_w2, j)
        else:
            h, u = residual_norm_matmul(h, ffn_out, odd_norm1[j], odd_ssm_w_in, j)
            y = ssm_core(u, (odd_lam_re[j], odd_lam_im[j], odd_log_dt[j], odd_b_re[j], odd_b_im[j],
                             odd_c_re[j], odd_c_im[j], odd_d_skip[j]))
            h, hn = gluproj_residual_norm(y, odd_glu_wa, odd_glu_wg, j, h, odd_norm2[j])
            route = router_top2(hn, odd_router[j])
            e1 = route[:, 0].astype(jnp.int32)
            e2 = route[:, 1].astype(jnp.int32)
            texp, trows, perm = _route_plan(e1, e2, MOE_TM, n_tiles)
            ys = moe_experts(hn, odd_moe_w1, odd_moe_w3, odd_moe_w2, j, texp, trows, perm, MOE_TM)
            if layer + 1 < depth:
                h, hn = combine_residual_norm(h, ys, route, even_norm1[j + 1], True, BF16)
            else:
                (out,) = combine_residual_norm(h, ys, route, final_norm, False, F32)
    return out.reshape(b, s, d)
```

```python
import functools
import math

import jax
import jax.numpy as jnp
from jax import lax
from jax.experimental import pallas as pl
from jax.experimental.pallas import tpu as pltpu

F32 = jnp.float32
BF16 = jnp.bfloat16

NORM_EPS = 1e-5
N_HEADS = 8
HEAD_DIM = 128
D_ATT = N_HEADS * HEAD_DIM
DILATIONS = (1, 4, 16)
BAND = 128
ATT_CHUNK = BAND * DILATIONS[-1]
ATT_UNROLL = 4
ALIBI_MAX_BIAS = 8.0
D_CONV = 1024
CONV_WIDTH = 31
CONV_HALO = 32
SSM_GROUP = 16
SSM_STATE = 64
SSM_CHUNK = 16
SSM_LANES = 128
N_EXPERTS = 8
TOP_K = 2
DMA_UNROLL = 8
MOE_ROW_STEP = 256

VMEM_LIMIT = 56 * 1024 * 1024


def _cparams(sem):
    return pltpu.CompilerParams(dimension_semantics=sem, vmem_limit_bytes=VMEM_LIMIT)


def _rms(x, g):
    return x * lax.rsqrt(jnp.mean(x * x, axis=-1, keepdims=True) + NORM_EPS) * g


def _rmsnorm_kernel(x_ref, g_ref, o_ref):
    o_ref[...] = _rms(x_ref[...], g_ref[...]).astype(o_ref.dtype)


def rmsnorm(x, g, out_dtype, tm=512):
    s, d = x.shape
    return pl.pallas_call(
        _rmsnorm_kernel,
        grid=(s // tm,),
        in_specs=[pl.BlockSpec((tm, d), lambda m: (m, 0)),
                  pl.BlockSpec((1, d), lambda m: (0, 0))],
        out_specs=pl.BlockSpec((tm, d), lambda m: (m, 0)),
        out_shape=jax.ShapeDtypeStruct((s, d), out_dtype),
        compiler_params=_cparams(("parallel",)),
    )(x, g.reshape(1, d))


def _matmul_kernel(x_ref, w_ref, o_ref, wb_ref):
    @pl.when(pl.program_id(1) == 0)
    def _():
        wb_ref[...] = w_ref[...].astype(BF16)

    o_ref[...] = jnp.dot(x_ref[...], wb_ref[...], preferred_element_type=F32).astype(o_ref.dtype)


def matmul(x, w, layer, out_dtype=F32, tm=2048, tn=512):
    s, k = x.shape
    n = w.shape[2]
    tn = min(tn, n)
    return pl.pallas_call(
        _matmul_kernel,
        grid=(n // tn, s // tm),
        in_specs=[pl.BlockSpec((tm, k), lambda j, m: (m, 0)),
                  pl.BlockSpec((None, k, tn), lambda j, m: (layer, 0, j))],
        out_specs=pl.BlockSpec((tm, tn), lambda j, m: (m, j)),
        out_shape=jax.ShapeDtypeStruct((s, n), out_dtype),
        scratch_shapes=[pltpu.VMEM((k, tn), BF16)],
        compiler_params=_cparams(("arbitrary", "arbitrary")),
    )(x, w)


def _resnorm_matmul_kernel(h_ref, y_ref, g_ref, w_ref, hnew_ref, o_ref, wb_ref):
    @pl.when(pl.program_id(0) == 0)
    def _():
        wb_ref[...] = w_ref[...].astype(BF16)

    hnew = h_ref[...] + y_ref[...]
    hnew_ref[...] = hnew
    xn = _rms(hnew, g_ref[...]).astype(BF16)
    o_ref[...] = jnp.dot(xn, wb_ref[...], preferred_element_type=F32).astype(o_ref.dtype)


def residual_norm_matmul(h, y, g, w, layer, tm=512):
    s, d = h.shape
    n = w.shape[2]
    row = lambda m: (m, 0)
    return pl.pallas_call(
        _resnorm_matmul_kernel,
        grid=(s // tm,),
        in_specs=[pl.BlockSpec((tm, d), row),
                  pl.BlockSpec((tm, d), row),
                  pl.BlockSpec((1, d), lambda m: (0, 0)),
                  pl.BlockSpec((None, d, n), lambda m: (layer, 0, 0), pipeline_mode=pl.Buffered(1))],
        out_specs=[pl.BlockSpec((tm, d), row), pl.BlockSpec((tm, n), row)],
        out_shape=[jax.ShapeDtypeStruct((s, d), F32), jax.ShapeDtypeStruct((s, n), F32)],
        scratch_shapes=[pltpu.VMEM((d, n), BF16)],
        compiler_params=_cparams(("arbitrary",)),
    )(h, y, g.reshape(1, d), w)


def _attn_kernel(q_ref, kp_ref, kc_ref, vp_ref, vc_ref, o_ref, kk, vv, osc, lsc):
    h = pl.program_id(0)
    c = pl.program_id(1)
    kk[0:ATT_CHUNK, :] = kp_ref[...]
    kk[ATT_CHUNK:2 * ATT_CHUNK, :] = kc_ref[...]
    vv[0:ATT_CHUNK, :] = vp_ref[...]
    vv[ATT_CHUNK:2 * ATT_CHUNK, :] = vc_ref[...]

    qi = lax.broadcasted_iota(jnp.int32, (BAND, 2 * BAND), 0)
    kj = lax.broadcasted_iota(jnp.int32, (BAND, 2 * BAND), 1)
    dist = BAND + qi - kj
    band = (dist >= 0) & (dist <= BAND)
    head = jnp.full((BAND, 2 * BAND), h + 1, jnp.int32).astype(F32)
    slope = jnp.exp2(-(ALIBI_MAX_BIAS / N_HEADS) * head)
    sdist = slope * dist.astype(F32)
    scale = HEAD_DIM ** -0.5
    blocks = ATT_CHUNK // BAND

    for g, d in enumerate(DILATIONS):
        bias = jnp.where(band, -sdist * float(d), -jnp.inf)
        bias_first = jnp.where(kj >= BAND, bias, -jnp.inf)

        def one_block(b, g=g, d=d, bias=bias, bias_first=bias_first):
            n = b // d
            r = b % d
            qstart = n * (BAND * d) + r
            kstart = ATT_CHUNK + (n - 1) * (BAND * d) + r
            q = q_ref[pl.ds(qstart, BAND, stride=d), :].astype(BF16)
            k = kk[pl.ds(kstart, 2 * BAND, stride=d), :].astype(BF16)
            v = vv[pl.ds(kstart, 2 * BAND, stride=d), :].astype(BF16)
            s = lax.dot_general(q, k, (((1,), (1,)), ((), ())), preferred_element_type=F32) * scale
            s = s + jnp.where((c == 0) & (n == 0), bias_first, bias)
            m = jnp.max(s, axis=-1, keepdims=True)
            p = jnp.exp(s - m)
            den = jnp.sum(p, axis=-1, keepdims=True)
            o = jnp.dot(p.astype(BF16), v, preferred_element_type=F32) / den
            lse = m + jnp.log(den)
            osc[g, pl.ds(qstart, BAND, stride=d), :] = o
            lsc[g, pl.ds(qstart, BAND, stride=d), :] = jnp.broadcast_to(lse, (BAND, HEAD_DIM))

        def body(i, carry, one_block=one_block):
            for u in range(ATT_UNROLL):
                one_block(i * ATT_UNROLL + u)
            return carry

        lax.fori_loop(0, blocks // ATT_UNROLL, body, 0)

    l0, l1, l2 = lsc[0], lsc[1], lsc[2]
    m = jnp.maximum(jnp.maximum(l0, l1), l2)
    w0, w1, w2 = jnp.exp(l0 - m), jnp.exp(l1 - m), jnp.exp(l2 - m)
    att = (w0 * osc[0] + w1 * osc[1] + w2 * osc[2]) / (w0 + w1 + w2)
    o_ref[...] = att.astype(o_ref.dtype)


def dilated_attention(proj):
    s = proj.shape[0]
    blk = (ATT_CHUNK, HEAD_DIM)
    prev = lambda c: jnp.maximum(c - 1, 0)
    return pl.pallas_call(
        _attn_kernel,
        grid=(N_HEADS, s // ATT_CHUNK),
        in_specs=[pl.BlockSpec(blk, lambda h, c: (c, h)),
                  pl.BlockSpec(blk, lambda h, c: (prev(c), N_HEADS + h)),
                  pl.BlockSpec(blk, lambda h, c: (c, N_HEADS + h)),
                  pl.BlockSpec(blk, lambda h, c: (prev(c), 2 * N_HEADS + h)),
                  pl.BlockSpec(blk, lambda h, c: (c, 2 * N_HEADS + h))],
        out_specs=pl.BlockSpec(blk, lambda h, c: (c, h)),
        out_shape=jax.ShapeDtypeStruct((s, D_ATT), BF16),
        scratch_shapes=[pltpu.VMEM((2 * ATT_CHUNK, HEAD_DIM), F32),
                        pltpu.VMEM((2 * ATT_CHUNK, HEAD_DIM), F32),
                        pltpu.VMEM((len(DILATIONS), ATT_CHUNK, HEAD_DIM), F32),
                        pltpu.VMEM((len(DILATIONS), ATT_CHUNK, HEAD_DIM), F32)],
        compiler_params=_cparams(("parallel", "parallel")),
    )(proj, proj, proj, proj, proj)


def _conv_kernel(val_ref, gate_ref, hval_ref, hgate_ref, w_ref, b_ref, lg_ref, lb_ref, o_ref, glu):
    m = pl.program_id(0)
    tm = val_ref.shape[0]
    halo = hval_ref[...] * jax.nn.sigmoid(hgate_ref[...])
    glu[0:CONV_HALO, :] = jnp.where(m > 0, halo, 0.0)
    glu[CONV_HALO:CONV_HALO + tm, :] = val_ref[...] * jax.nn.sigmoid(gate_ref[...])
    first = CONV_HALO - (CONV_WIDTH - 1)

    acc = jnp.broadcast_to(b_ref[...], (tm, D_CONV))
    for j in range(CONV_WIDTH):
        acc = acc + w_ref[j:j + 1, :] * glu[pl.ds(first + j, tm), :]
    mu = jnp.mean(acc, axis=-1, keepdims=True)
    cen = acc - mu
    var = jnp.mean(cen * cen, axis=-1, keepdims=True)
    y = cen * lax.rsqrt(var + NORM_EPS) * lg_ref[...] + lb_ref[...]
    o_ref[...] = (y * jax.nn.sigmoid(y)).astype(o_ref.dtype)


def conv_module(proj, conv_w, conv_b, ln_g, ln_b, tm=512):
    s = proj.shape[0]
    vcol = 3 * D_ATT // D_CONV
    gcol = vcol + 1
    per = tm // CONV_HALO
    hrow = lambda m: jnp.maximum(m * per - 1, 0)
    vec = pl.BlockSpec((1, D_CONV), lambda m: (0, 0))
    return pl.pallas_call(
        _conv_kernel,
        grid=(s // tm,),
        in_specs=[pl.BlockSpec((tm, D_CONV), lambda m: (m, vcol)),
                  pl.BlockSpec((tm, D_CONV), lambda m: (m, gcol)),
                  pl.BlockSpec((CONV_HALO, D_CONV), lambda m: (hrow(m), vcol)),
                  pl.BlockSpec((CONV_HALO, D_CONV), lambda m: (hrow(m), gcol)),
                  pl.BlockSpec((CONV_WIDTH, D_CONV), lambda m: (0, 0)),
                  vec, vec, vec],
        out_specs=pl.BlockSpec((tm, D_CONV), lambda m: (m, 0)),
        out_shape=jax.ShapeDtypeStruct((s, D_CONV), BF16),
        scratch_shapes=[pltpu.VMEM((CONV_HALO + tm, D_CONV), F32)],
        compiler_params=_cparams(("parallel",)),
    )(proj, proj, proj, proj, conv_w, conv_b.reshape(1, -1), ln_g.reshape(1, -1), ln_b.reshape(1, -1))


def _outproj_kernel(att_ref, cv_ref, w_ref, h_ref, g_ref, hnew_ref, hn_ref, wb_ref):
    @pl.when(pl.program_id(0) == 0)
    def _():
        wb_ref[...] = w_ref[...].astype(BF16)

    y = jnp.dot(att_ref[...], wb_ref[0:D_ATT, :], preferred_element_type=F32)
    y = y + jnp.dot(cv_ref[...], wb_ref[D_ATT:D_ATT + D_CONV, :], preferred_element_type=F32)
    hnew = h_ref[...] + y
    hnew_ref[...] = hnew
    hn_ref[...] = _rms(hnew, g_ref[...]).astype(hn_ref.dtype)


def outproj_residual_norm(att, cv, w_out, layer, h, g, tm=512):
    s, d = h.shape
    kin = w_out.shape[1]
    row = lambda m: (m, 0)
    fixed = lambda m: (0, 0)
    return pl.pallas_call(
        _outproj_kernel,
        grid=(s // tm,),
        in_specs=[pl.BlockSpec((tm, D_ATT), row),
                  pl.BlockSpec((tm, D_CONV), row),
                  pl.BlockSpec((None, kin, d), lambda m: (layer, 0, 0), pipeline_mode=pl.Buffered(1)),
                  pl.BlockSpec((tm, d), row),
                  pl.BlockSpec((1, d), fixed)],
        out_specs=[pl.BlockSpec((tm, d), row), pl.BlockSpec((tm, d), row)],
        out_shape=[jax.ShapeDtypeStruct((s, d), F32), jax.ShapeDtypeStruct((s, d), BF16)],
        scratch_shapes=[pltpu.VMEM((kin, d), BF16)],
        compiler_params=_cparams(("arbitrary",)),
    )(att, cv, w_out, h, g.reshape(1, d))


def _gluproj_kernel(y_ref, wa_ref, wg_ref, h_ref, g_ref, hnew_ref, hn_ref, wab_ref, wgb_ref):
    @pl.when(pl.program_id(0) == 0)
    def _():
        wab_ref[...] = wa_ref[...].astype(BF16)
        wgb_ref[...] = wg_ref[...].astype(BF16)

    y = y_ref[...].astype(BF16)
    a = jnp.dot(y, wab_ref[...], preferred_element_type=F32)
    gt = jnp.dot(y, wgb_ref[...], preferred_element_type=F32)
    hnew = h_ref[...] + a * jax.nn.sigmoid(gt)
    hnew_ref[...] = hnew
    hn_ref[...] = _rms(hnew, g_ref[...]).astype(hn_ref.dtype)


def gluproj_residual_norm(y, wa, wg, layer, h, g, tm=256):
    s, d = h.shape
    kin = wa.shape[1]
    row = lambda m: (m, 0)
    fixed = lambda m: (0, 0)
    once = lambda m: (layer, 0, 0)
    return pl.pallas_call(
        _gluproj_kernel,
        grid=(s // tm,),
        in_specs=[pl.BlockSpec((tm, kin), row),
                  pl.BlockSpec((None, kin, d), once, pipeline_mode=pl.Buffered(1)),
                  pl.BlockSpec((None, kin, d), once, pipeline_mode=pl.Buffered(1)),
                  pl.BlockSpec((tm, d), row),
                  pl.BlockSpec((1, d), fixed)],
        out_specs=[pl.BlockSpec((tm, d), row), pl.BlockSpec((tm, d), row)],
        out_shape=[jax.ShapeDtypeStruct((s, d), F32), jax.ShapeDtypeStruct((s, d), F32)],
        scratch_shapes=[pltpu.VMEM((kin, d), BF16), pltpu.VMEM((kin, d), BF16)],
        compiler_params=_cparams(("arbitrary",)),
    )(y, wa, wg, h, g.reshape(1, d))


def _swiglu_step(xb, w1_ref, w3_ref, w2_ref):
    a = jnp.dot(xb, w1_ref[...].astype(BF16), preferred_element_type=F32)
    b = jnp.dot(xb, w3_ref[...].astype(BF16), preferred_element_type=F32)
    p = (a * jax.nn.sigmoid(a) * b).astype(BF16)
    return jnp.dot(p, w2_ref[...].astype(BF16), preferred_element_type=F32)


def _ffn_kernel(x_ref, w1_ref, w3_ref, w2_ref, y_ref):
    @pl.when(pl.program_id(1) == 0)
    def _():
        y_ref[...] = jnp.zeros_like(y_ref)

    y_ref[...] += _swiglu_step(x_ref[...], w1_ref, w3_ref, w2_ref)


def ffn(x, w1, w3, w2, layer, tm=1024, tf=256):
    s, d = x.shape
    dff = w1.shape[2]
    row = lambda m, f: (m, 0)
    return pl.pallas_call(
        _ffn_kernel,
        grid=(s // tm, dff // tf),
        in_specs=[pl.BlockSpec((tm, d), row),
                  pl.BlockSpec((None, d, tf), lambda m, f: (layer, 0, f)),
                  pl.BlockSpec((None, d, tf), lambda m, f: (layer, 0, f)),
                  pl.BlockSpec((None, tf, d), lambda m, f: (layer, f, 0))],
        out_specs=pl.BlockSpec((tm, d), row),
        out_shape=jax.ShapeDtypeStruct((s, d), F32),
        compiler_params=_cparams(("parallel", "arbitrary")),
    )(x, w1, w3, w2)


def _ssm_tables(lam_re, lam_im, log_dt, b_re, b_im, c_re, c_im, d_skip, n_doublings):
    L = SSM_CHUNK
    G, P = lam_re.shape
    dt = jnp.exp(log_dt)[:, None]
    mag = jnp.exp(lam_re * dt)
    a_re = mag * jnp.cos(lam_im * dt)
    a_im = mag * jnp.sin(lam_im * dt)
    inv = 1.0 / (lam_re * lam_re + lam_im * lam_im)
    f_re = ((a_re - 1.0) * lam_re + a_im * lam_im) * inv
    f_im = (a_im * lam_re - (a_re - 1.0) * lam_im) * inv
    bb_re = f_re[..., None] * b_re - f_im[..., None] * b_im
    bb_im = f_re[..., None] * b_im + f_im[..., None] * b_re

    pw_re = [jnp.ones_like(a_re)]
    pw_im = [jnp.zeros_like(a_im)]
    for _ in range(L):
        pr, pi = pw_re[-1], pw_im[-1]
        pw_re.append(pr * a_re - pi * a_im)
        pw_im.append(pr * a_im + pi * a_re)
    pw_re = jnp.stack(pw_re)
    pw_im = jnp.stack(pw_im)

    ab_re = pw_re[:L, :, :, None] * bb_re[None] - pw_im[:L, :, :, None] * bb_im[None]
    ab_im = pw_re[:L, :, :, None] * bb_im[None] + pw_im[:L, :, :, None] * bb_re[None]
    hi = lax.Precision.HIGHEST
    kern = (jnp.einsum('gcp,tgpd->tgcd', c_re, ab_re, precision=hi)
            - jnp.einsum('gcp,tgpd->tgcd', c_im, ab_im, precision=hi))
    C = SSM_GROUP
    Q = SSM_LANES // C
    O = G // Q
    kern = kern.at[0].add(jnp.eye(C, dtype=F32)[None] * d_skip.reshape(G, C)[:, :, None])

    kc = kern.reshape(L, O, Q, C, C).transpose(1, 4, 0, 2, 3).reshape(O, C, L * Q * C)

    def by_state(t):
        return jnp.stack(t, axis=3).reshape(O, L, C, 2 * Q * P)

    bc = by_state([ab[::-1].reshape(L, O, Q, P, C).transpose(1, 0, 4, 2, 3) for ab in (ab_re, ab_im)])

    ca_re = c_re[None] * pw_re[1:, :, None, :] - c_im[None] * pw_im[1:, :, None, :]
    ca_im = c_re[None] * pw_im[1:, :, None, :] + c_im[None] * pw_re[1:, :, None, :]
    cc = by_state([ca.reshape(L, O, Q, C, P).transpose(1, 0, 3, 2, 4) for ca in (ca_re, -ca_im)])

    sq_re, sq_im = pw_re[L].reshape(O, Q * P), pw_im[L].reshape(O, Q * P)
    same, cross = [], []
    for _ in range(n_doublings):
        same.append(jnp.concatenate([sq_re, sq_re], axis=-1))
        cross.append(jnp.concatenate([-sq_im, sq_im], axis=-1))
        sq_re, sq_im = sq_re * sq_re - sq_im * sq_im, 2.0 * sq_re * sq_im
    return kc, bc, cc, jnp.stack(same, axis=1), jnp.stack(cross, axis=1)


def _ssm_kernel(u_ref, kc_ref, bc_ref, cc_ref, same_ref, cross_ref, y_ref, yacc, kcat):
    L = SSM_CHUNK
    nc = yacc.shape[0]
    n_state = same_ref.shape[1]
    half = n_state // 2
    groups = SSM_LANES // SSM_GROUP

    def spread(table, col_group):
        rows = lax.broadcasted_iota(jnp.int32, (SSM_LANES, table.shape[1]), 0)
        cols = lax.broadcasted_iota(jnp.int32, (SSM_LANES, table.shape[1]), 1)
        tiled = jnp.concatenate([table] * groups, axis=0)
        return jnp.where(rows // SSM_GROUP == col_group(cols), tiled, 0.0).astype(BF16)

    state_group = lambda cols: (cols % half) // SSM_STATE
    lane_group = lambda cols: (cols % SSM_LANES) // SSM_GROUP

    kcat[...] = spread(kc_ref[...], lane_group)
    yacc[...] = jnp.zeros_like(yacc)
    x = jnp.zeros((nc, n_state), F32)
    for j in range(L):
        uj = u_ref[pl.ds(j, nc, stride=L), :].astype(BF16)
        x = x + jnp.dot(uj, spread(bc_ref[j], state_group), preferred_element_type=F32)
        width = (L - j) * SSM_LANES
        yacc[:, j * SSM_LANES:] += jnp.dot(uj, kcat[:, :width], preferred_element_type=F32)
    row = lax.broadcasted_iota(jnp.int32, x.shape, 0)
    for k in range(same_ref.shape[0]):
        shift = 1 << k
        sh = jnp.where(row >= shift, pltpu.roll(x, shift, 0), 0.0)
        x = x + same_ref[k:k + 1, :] * sh + cross_ref[k:k + 1, :] * pltpu.roll(sh, half, 1)
    prev = jnp.where(row >= 1, pltpu.roll(x, 1, 0), 0.0).astype(BF16)
    for s in range(L):
        carried = lax.dot_general(prev, spread(cc_ref[s], state_group), (((1,), (1,)), ((), ())),
                                  preferred_element_type=F32)
        y_ref[pl.ds(s, nc, stride=L), :] = jax.nn.gelu(yacc[:, s * SSM_LANES:(s + 1) * SSM_LANES] + carried)


def ssm_core(u, params):
    s, width = u.shape
    L = SSM_CHUNK
    nc = s // L
    n_doublings = max(1, (nc - 1).bit_length())
    kc, bc, cc, same, cross = _ssm_tables(*params, n_doublings=n_doublings)
    n_blocks = width // SSM_LANES
    n_state = same.shape[2]
    blk3 = lambda o: (o, 0, 0)
    blk4 = lambda o: (o, 0, 0, 0)
    return pl.pallas_call(
        _ssm_kernel,
        grid=(n_blocks,),
        in_specs=[pl.BlockSpec((s, SSM_LANES), lambda o: (0, o)),
                  pl.BlockSpec((None, SSM_GROUP, L * SSM_LANES), blk3),
                  pl.BlockSpec((None, L, SSM_GROUP, n_state), blk4),
                  pl.BlockSpec((None, L, SSM_GROUP, n_state), blk4),
                  pl.BlockSpec((None, n_doublings, n_state), blk3),
                  pl.BlockSpec((None, n_doublings, n_state), blk3)],
        out_specs=pl.BlockSpec((s, SSM_LANES), lambda o: (0, o)),
        out_shape=jax.ShapeDtypeStruct((s, width), F32),
        scratch_shapes=[pltpu.VMEM((nc, L * SSM_LANES), F32),
                        pltpu.VMEM((SSM_LANES, L * SSM_LANES), BF16)],
        compiler_params=_cparams(("parallel",)),
    )(u, kc, bc, cc, same, cross)


def _router_kernel(x_ref, r_ref, o_ref):
    logits = jnp.dot(x_ref[...], r_ref[...], preferred_element_type=F32, precision=lax.Precision.HIGHEST)
    lane = lax.broadcasted_iota(jnp.int32, logits.shape, 1)
    logits = jnp.where(lane < N_EXPERTS, logits, -jnp.inf)
    big = jnp.int32(logits.shape[1])
    v1 = jnp.max(logits, axis=-1, keepdims=True)
    i1 = jnp.min(jnp.where(logits == v1, lane, big), axis=-1, keepdims=True)
    rest = jnp.where(lane == i1, -jnp.inf, logits)
    v2 = jnp.max(rest, axis=-1, keepdims=True)
    i2 = jnp.min(jnp.where(rest == v2, lane, big), axis=-1, keepdims=True)
    e2 = jnp.exp(v2 - v1)
    g1 = 1.0 / (1.0 + e2)
    g2 = e2 / (1.0 + e2)
    out = jnp.where(lane == 0, i1.astype(F32), 0.0)
    out = jnp.where(lane == 1, i2.astype(F32), out)
    out = jnp.where(lane == 2, g1, out)
    out = jnp.where(lane == 3, g2, out)
    o_ref[...] = out


def router_top2(x, router, tm=512):
    s, d = x.shape
    lanes = 128
    rp = jnp.zeros((d, lanes), F32).at[:, :N_EXPERTS].set(router)
    return pl.pallas_call(
        _router_kernel,
        grid=(s // tm,),
        in_specs=[pl.BlockSpec((tm, d), lambda m: (m, 0)),
                  pl.BlockSpec((d, lanes), lambda m: (0, 0))],
        out_specs=pl.BlockSpec((tm, lanes), lambda m: (m, 0)),
        out_shape=jax.ShapeDtypeStruct((s, lanes), F32),
        compiler_params=_cparams(("parallel",)),
    )(x, rp)


def _moe_kernel(texp_ref, trows_ref, perm_ref,
                x_hbm, w1_ref, w3_ref, w2_ref, y_hbm, xb, acc, gsem, ssem):
    t = pl.program_id(0)
    f = pl.program_id(1)
    tm = acc.shape[0]
    n_tok = x_hbm.shape[0]
    rows = trows_ref[t]
    base = t * tm

    def gather_copy(i):
        a = perm_ref[base + i]
        tok = jnp.where(a >= n_tok, a - n_tok, a)
        return pltpu.make_async_copy(x_hbm.at[pl.ds(tok, 1), :], acc.at[pl.ds(i, 1), :], gsem)

    def scatter_copy(i):
        return pltpu.make_async_copy(acc.at[pl.ds(i, 1), :], y_hbm.at[pl.ds(perm_ref[base + i], 1), :], ssem)

    def for_rows(fn):
        groups = rows // DMA_UNROLL

        def group_body(gi, carry):
            for u in range(DMA_UNROLL):
                fn(gi * DMA_UNROLL + u)
            return carry

        def row_body(i, carry):
            fn(i)
            return carry

        lax.fori_loop(0, groups, group_body, 0)
        lax.fori_loop(groups * DMA_UNROLL, rows, row_body, 0)

    def start_all(make):
        for_rows(lambda i: make(i).start())

    def wait_all(make):
        for_rows(lambda i: make(i).wait())

    @pl.when(rows > 0)
    def _():
        @pl.when(f == 0)
        def _():
            acc[...] = jnp.zeros_like(acc)
            start_all(gather_copy)
            wait_all(gather_copy)
            xb[...] = acc[...].astype(BF16)
            acc[...] = jnp.zeros_like(acc)

        lo = 0
        for m in _moe_row_buckets(tm):
            @pl.when((rows > lo) & (rows <= m))
            def _(m=m):
                acc[0:m, :] += _swiglu_step(xb[0:m, :], w1_ref, w3_ref, w2_ref)
            lo = m

        @pl.when(f == pl.num_programs(1) - 1)
        def _():
            start_all(scatter_copy)
            wait_all(scatter_copy)


def _moe_row_buckets(tm):
    return tuple(range(2 * MOE_ROW_STEP, tm, MOE_ROW_STEP)) + (tm,)


def moe_experts(x, w1, w3, w2, layer, tile_expert, tile_rows, perm, tm, tf=256):
    s, d = x.shape
    dff = w1.shape[3]
    n_tiles = tile_expert.shape[0]
    nf = dff // tf
    last = nf - 1

    def fsel(t, f, trows):
        return jnp.where(trows[t] > 0, f, last)

    grid_spec = pltpu.PrefetchScalarGridSpec(
        num_scalar_prefetch=3,
        grid=(n_tiles, nf),
        in_specs=[pl.BlockSpec(memory_space=pl.ANY),
                  pl.BlockSpec((None, None, d, tf), lambda t, f, te, tr, pm: (layer, te[t], 0, fsel(t, f, tr))),
                  pl.BlockSpec((None, None, d, tf), lambda t, f, te, tr, pm: (layer, te[t], 0, fsel(t, f, tr))),
                  pl.BlockSpec((None, None, tf, d), lambda t, f, te, tr, pm: (layer, te[t], fsel(t, f, tr), 0))],
        out_specs=pl.BlockSpec(memory_space=pl.ANY),
        scratch_shapes=[pltpu.VMEM((tm, d), BF16),
                        pltpu.VMEM((tm, d), F32),
                        pltpu.SemaphoreType.DMA(()),
                        pltpu.SemaphoreType.DMA(())],
    )
    return pl.pallas_call(
        _moe_kernel,
        grid_spec=grid_spec,
        out_shape=jax.ShapeDtypeStruct((TOP_K * s, d), F32),
        compiler_params=_cparams(("arbitrary", "arbitrary")),
    )(tile_expert, tile_rows, perm, x, w1, w3, w2)


def _route_plan(e1, e2, tm, n_tiles):
    s = e1.shape[0]
    experts = jnp.concatenate([e1, e2])
    onehot = (experts[:, None] == jnp.arange(N_EXPERTS, dtype=jnp.int32)[None, :]).astype(jnp.int32)
    csum = jnp.cumsum(onehot, axis=0)
    counts = csum[-1]
    tiles_per = (counts + tm - 1) // tm
    tile_end = jnp.cumsum(tiles_per)
    tile_start = tile_end - tiles_per
    per_tile = (counts + jnp.maximum(tiles_per, 1) - 1) // jnp.maximum(tiles_per, 1)
    rank = jnp.sum(onehot * (csum - 1), axis=1)
    own_start = jnp.sum(onehot * tile_start[None, :], axis=1)
    own_per_tile = jnp.maximum(jnp.sum(onehot * per_tile[None, :], axis=1), 1)
    pos = (own_start + rank // own_per_tile) * tm + rank % own_per_tile
    n_rows = n_tiles * tm
    perm = jnp.zeros((n_rows,), jnp.int32).at[pos].set(jnp.arange(TOP_K * s, dtype=jnp.int32))
    tid = jnp.arange(n_tiles, dtype=jnp.int32)
    texp = jnp.minimum(jnp.sum((tid[:, None] >= tile_end[None, :]).astype(jnp.int32), axis=1), N_EXPERTS - 1)
    used = tid < tile_end[-1]
    local = tid - tile_start[texp]
    trows = jnp.where(used, jnp.clip(counts[texp] - local * per_tile[texp], 0, per_tile[texp]), 0).astype(jnp.int32)
    last_used = texp[jnp.maximum(tile_end[-1] - 1, 0)]
    texp = jnp.where(used, texp, last_used).astype(jnp.int32)
    return texp, trows, perm


def _combine_kernel(h_ref, y1_ref, y2_ref, gt_ref, g_ref, *out_refs, want_h):
    gates = gt_ref[...]
    hnew = h_ref[...] + gates[:, 2:3] * y1_ref[...] + gates[:, 3:4] * y2_ref[...]
    if want_h:
        out_refs[0][...] = hnew
    out_refs[-1][...] = _rms(hnew, g_ref[...]).astype(out_refs[-1].dtype)


def combine_residual_norm(h, y, route, g, want_h, norm_dtype, tm=512):
    s, d = h.shape
    nb = s // tm
    row = lambda m: (m, 0)
    outs_spec = [pl.BlockSpec((tm, d), row)]
    outs_shape = [jax.ShapeDtypeStruct((s, d), norm_dtype)]
    if want_h:
        outs_spec = [pl.BlockSpec((tm, d), row)] + outs_spec
        outs_shape = [jax.ShapeDtypeStruct((s, d), F32)] + outs_shape
    return pl.pallas_call(
        functools.partial(_combine_kernel, want_h=want_h),
        grid=(nb,),
        in_specs=[pl.BlockSpec((tm, d), row),
                  pl.BlockSpec((tm, d), row),
                  pl.BlockSpec((tm, d), lambda m: (m + nb, 0)),
                  pl.BlockSpec((tm, route.shape[1]), row),
                  pl.BlockSpec((1, d), lambda m: (0, 0))],
        out_specs=outs_spec,
        out_shape=outs_shape,
        compiler_params=_cparams(("parallel",)),
    )(h, y, y, route, g.reshape(1, d))


MOE_TM = 2176


def kernel(x, even_norm1, even_w_in, even_conv_w, even_conv_b, even_cln_g, even_cln_b, even_w_out, even_norm2, even_ffn_w1, even_ffn_w3, even_ffn_w2, odd_norm1, odd_ssm_w_in, odd_lam_re, odd_lam_im, odd_log_dt, odd_b_re, odd_b_im, odd_c_re, odd_c_im, odd_d_skip, odd_glu_wa, odd_glu_wg, odd_norm2, odd_router, odd_moe_w1, odd_moe_w3, odd_moe_w2, final_norm):
    b, s, d = x.shape
    assert b == 1
    depth = even_norm1.shape[0] + odd_norm1.shape[0]
    assert even_norm1.shape[0] == odd_norm1.shape[0]
    h = x.reshape(s, d)
    hn = rmsnorm(h, even_norm1[0], BF16)
    n_tiles = (TOP_K * s) // MOE_TM + N_EXPERTS
    out = None
    ffn_out = None
    for layer in range(depth):
        j = layer // 2
        if layer % 2 == 0:
            proj = matmul(hn, even_w_in, j)
            att = dilated_attention(proj)
            cv = conv_module(proj, even_conv_w[j], even_conv_b[j], even_cln_g[j], even_cln_b[j])
            h, hn = outproj_residual_norm(att, cv, even_w_out, j, h, even_norm2[j])
            ffn_out = ffn(hn, even_ffn_w1, even_ffn_w3, even_ffn_w2, j)
        else:
            h, u = residual_norm_matmul(h, ffn_out, odd_norm1[j], odd_ssm_w_in, j)
            y = ssm_core(u, (odd_lam_re[j], odd_lam_im[j], odd_log_dt[j], odd_b_re[j], odd_b_im[j],
                             odd_c_re[j], odd_c_im[j], odd_d_skip[j]))
            h, hn = gluproj_residual_norm(y, odd_glu_wa, odd_glu_wg, j, h, odd_norm2[j])
            route = router_top2(hn, odd_router[j])
            e1 = route[:, 0].astype(jnp.int32)
            e2 = route[:, 1].astype(jnp.int32)
            texp, trows, perm = _route_plan(e1, e2, MOE_TM, n_tiles)
            ys = moe_experts(hn, odd_moe_w1, odd_moe_w3, odd_moe_w2, j, texp, trows, perm, MOE_TM)
            if layer + 1 < depth:
                h, hn = combine_residual_norm(h, ys, route, even_norm1[j + 1], True, BF16)
            else:
                (out,) = combine_residual_norm(h, ys, route, final_norm, False, F32)
    return out.reshape(b, s, d)
```

```python
import functools
import math

import jax
import jax.numpy as jnp
from jax import lax
from jax.experimental import pallas as pl
from jax.experimental.pallas import tpu as pltpu

F32 = jnp.float32
BF16 = jnp.bfloat16

NORM_EPS = 1e-5
N_HEADS = 8
HEAD_DIM = 128
D_ATT = N_HEADS * HEAD_DIM
DILATIONS = (1, 4, 16)
BAND = 128
ATT_CHUNK = BAND * DILATIONS[-1]
ATT_UNROLL = {1: 16, 4: 16, 16: 8}
ALIBI_MAX_BIAS = 8.0
D_CONV = 1024
CONV_WIDTH = 31
CONV_HALO = 32
SSM_GROUP = 16
SSM_STATE = 64
SSM_CHUNK = 16
SSM_LANES = 128
N_EXPERTS = 8
TOP_K = 2
DMA_UNROLL = 8
MOE_ROW_STEP = 128

VMEM_LIMIT = 56 * 1024 * 1024


def _cparams(sem):
    return pltpu.CompilerParams(dimension_semantics=sem, vmem_limit_bytes=VMEM_LIMIT)


def _rms(x, g):
    return x * lax.rsqrt(jnp.mean(x * x, axis=-1, keepdims=True) + NORM_EPS) * g


def _rmsnorm_kernel(x_ref, g_ref, o_ref):
    o_ref[...] = _rms(x_ref[...], g_ref[...]).astype(o_ref.dtype)


def rmsnorm(x, g, out_dtype, tm=512):
    s, d = x.shape
    return pl.pallas_call(
        _rmsnorm_kernel,
        grid=(s // tm,),
        in_specs=[pl.BlockSpec((tm, d), lambda m: (m, 0)),
                  pl.BlockSpec((1, d), lambda m: (0, 0))],
        out_specs=pl.BlockSpec((tm, d), lambda m: (m, 0)),
        out_shape=jax.ShapeDtypeStruct((s, d), out_dtype),
        compiler_params=_cparams(("parallel",)),
    )(x, g.reshape(1, d))


def _matmul_kernel(x_ref, w_ref, o_ref, wb_ref):
    @pl.when(pl.program_id(1) == 0)
    def _():
        wb_ref[...] = w_ref[...].astype(BF16)

    o_ref[...] = jnp.dot(x_ref[...], wb_ref[...], preferred_element_type=F32).astype(o_ref.dtype)


def matmul(x, w, layer, out_dtype=F32, tm=2048, tn=512):
    s, k = x.shape
    n = w.shape[2]
    tn = min(tn, n)
    return pl.pallas_call(
        _matmul_kernel,
        grid=(n // tn, s // tm),
        in_specs=[pl.BlockSpec((tm, k), lambda j, m: (m, 0)),
                  pl.BlockSpec((None, k, tn), lambda j, m: (layer, 0, j))],
        out_specs=pl.BlockSpec((tm, tn), lambda j, m: (m, j)),
        out_shape=jax.ShapeDtypeStruct((s, n), out_dtype),
        scratch_shapes=[pltpu.VMEM((k, tn), BF16)],
        compiler_params=_cparams(("arbitrary", "arbitrary")),
    )(x, w)


def _resnorm_matmul_kernel(h_ref, y_ref, g_ref, w_ref, hnew_ref, o_ref, wb_ref):
    @pl.when(pl.program_id(0) == 0)
    def _():
        wb_ref[...] = w_ref[...].astype(BF16)

    hnew = h_ref[...] + y_ref[...]
    hnew_ref[...] = hnew
    xn = _rms(hnew, g_ref[...]).astype(BF16)
    o_ref[...] = jnp.dot(xn, wb_ref[...], preferred_element_type=F32).astype(o_ref.dtype)


def residual_norm_matmul(h, y, g, w, layer, tm=512):
    s, d = h.shape
    n = w.shape[2]
    row = lambda m: (m, 0)
    return pl.pallas_call(
        _resnorm_matmul_kernel,
        grid=(s // tm,),
        in_specs=[pl.BlockSpec((tm, d), row),
                  pl.BlockSpec((tm, d), row),
                  pl.BlockSpec((1, d), lambda m: (0, 0)),
                  pl.BlockSpec((None, d, n), lambda m: (layer, 0, 0), pipeline_mode=pl.Buffered(1))],
        out_specs=[pl.BlockSpec((tm, d), row), pl.BlockSpec((tm, n), row)],
        out_shape=[jax.ShapeDtypeStruct((s, d), F32), jax.ShapeDtypeStruct((s, n), F32)],
        scratch_shapes=[pltpu.VMEM((d, n), BF16)],
        compiler_params=_cparams(("arbitrary",)),
    )(h, y, g.reshape(1, d), w)


def _attn_kernel(q_ref, kp_ref, kc_ref, vp_ref, vc_ref, o_ref, kk, vv, osc, lsc):
    h = pl.program_id(0)
    c = pl.program_id(1)
    kk[0:ATT_CHUNK, :] = kp_ref[...]
    kk[ATT_CHUNK:2 * ATT_CHUNK, :] = kc_ref[...]
    vv[0:ATT_CHUNK, :] = vp_ref[...]
    vv[ATT_CHUNK:2 * ATT_CHUNK, :] = vc_ref[...]

    qi = lax.broadcasted_iota(jnp.int32, (BAND, 2 * BAND), 0)
    kj = lax.broadcasted_iota(jnp.int32, (BAND, 2 * BAND), 1)
    dist = BAND + qi - kj
    band = (dist >= 0) & (dist <= BAND)
    head = jnp.full((BAND, 2 * BAND), h + 1, jnp.int32).astype(F32)
    slope = jnp.exp2(-(ALIBI_MAX_BIAS / N_HEADS) * head)
    sdist = slope * dist.astype(F32)
    scale = HEAD_DIM ** -0.5
    blocks = ATT_CHUNK // BAND

    for g, d in enumerate(DILATIONS):
        bias = jnp.where(band, -sdist * float(d), -jnp.inf)
        bias_first = jnp.where(kj >= BAND, bias, -jnp.inf)

        def one_block(b, g=g, d=d, bias=bias, bias_first=bias_first):
            n = b // d
            r = b % d
            qstart = n * (BAND * d) + r
            kstart = ATT_CHUNK + (n - 1) * (BAND * d) + r
            q = q_ref[pl.ds(qstart, BAND, stride=d), :].astype(BF16)
            k = kk[pl.ds(kstart, 2 * BAND, stride=d), :].astype(BF16)
            v = vv[pl.ds(kstart, 2 * BAND, stride=d), :].astype(BF16)
            s = lax.dot_general(q, k, (((1,), (1,)), ((), ())), preferred_element_type=F32) * scale
            s = s + jnp.where((c == 0) & (n == 0), bias_first, bias)
            m = jnp.max(s, axis=-1, keepdims=True)
            p = jnp.exp(s - m)
            den = jnp.sum(p, axis=-1, keepdims=True)
            o = jnp.dot(p.astype(BF16), v, preferred_element_type=F32) / den
            lse = m + jnp.log(den)
            osc[g, pl.ds(qstart, BAND, stride=d), :] = o
            lsc[g, pl.ds(qstart, BAND, stride=d), :] = jnp.broadcast_to(lse, (BAND, HEAD_DIM))

        unroll = ATT_UNROLL[d]

        def body(i, carry, one_block=one_block, unroll=unroll):
            for u in range(unroll):
                one_block(i * unroll + u)
            return carry

        lax.fori_loop(0, blocks // unroll, body, 0)

    l0, l1, l2 = lsc[0], lsc[1], lsc[2]
    m = jnp.maximum(jnp.maximum(l0, l1), l2)
    w0, w1, w2 = jnp.exp(l0 - m), jnp.exp(l1 - m), jnp.exp(l2 - m)
    att = (w0 * osc[0] + w1 * osc[1] + w2 * osc[2]) / (w0 + w1 + w2)
    o_ref[...] = att.astype(o_ref.dtype)


def dilated_attention(proj):
    s = proj.shape[0]
    blk = (ATT_CHUNK, HEAD_DIM)
    prev = lambda c: jnp.maximum(c - 1, 0)
    return pl.pallas_call(
        _attn_kernel,
        grid=(N_HEADS, s // ATT_CHUNK),
        in_specs=[pl.BlockSpec(blk, lambda h, c: (c, h)),
                  pl.BlockSpec(blk, lambda h, c: (prev(c), N_HEADS + h)),
                  pl.BlockSpec(blk, lambda h, c: (c, N_HEADS + h)),
                  pl.BlockSpec(blk, lambda h, c: (prev(c), 2 * N_HEADS + h)),
                  pl.BlockSpec(blk, lambda h, c: (c, 2 * N_HEADS + h))],
        out_specs=pl.BlockSpec(blk, lambda h, c: (c, h)),
        out_shape=jax.ShapeDtypeStruct((s, D_ATT), BF16),
        scratch_shapes=[pltpu.VMEM((2 * ATT_CHUNK, HEAD_DIM), F32),
                        pltpu.VMEM((2 * ATT_CHUNK, HEAD_DIM), F32),
                        pltpu.VMEM((len(DILATIONS), ATT_CHUNK, HEAD_DIM), F32),
                        pltpu.VMEM((len(DILATIONS), ATT_CHUNK, HEAD_DIM), F32)],
        compiler_params=_cparams(("parallel", "parallel")),
    )(proj, proj, proj, proj, proj)


def _conv_kernel(val_ref, gate_ref, hval_ref, hgate_ref, w_ref, b_ref, lg_ref, lb_ref, o_ref, glu):
    m = pl.program_id(0)
    tm = val_ref.shape[0]
    halo = hval_ref[...] * jax.nn.sigmoid(hgate_ref[...])
    glu[0:CONV_HALO, :] = jnp.where(m > 0, halo, 0.0)
    glu[CONV_HALO:CONV_HALO + tm, :] = val_ref[...] * jax.nn.sigmoid(gate_ref[...])
    first = CONV_HALO - (CONV_WIDTH - 1)

    acc = jnp.broadcast_to(b_ref[...], (tm, D_CONV))
    for j in range(CONV_WIDTH):
        acc = acc + w_ref[j:j + 1, :] * glu[pl.ds(first + j, tm), :]
    mu = jnp.mean(acc, axis=-1, keepdims=True)
    cen = acc - mu
    var = jnp.mean(cen * cen, axis=-1, keepdims=True)
    y = cen * lax.rsqrt(var + NORM_EPS) * lg_ref[...] + lb_ref[...]
    o_ref[...] = (y * jax.nn.sigmoid(y)).astype(o_ref.dtype)


def conv_module(proj, conv_w, conv_b, ln_g, ln_b, tm=512):
    s = proj.shape[0]
    vcol = 3 * D_ATT // D_CONV
    gcol = vcol + 1
    per = tm // CONV_HALO
    hrow = lambda m: jnp.maximum(m * per - 1, 0)
    vec = pl.BlockSpec((1, D_CONV), lambda m: (0, 0))
    return pl.pallas_call(
        _conv_kernel,
        grid=(s // tm,),
        in_specs=[pl.BlockSpec((tm, D_CONV), lambda m: (m, vcol)),
                  pl.BlockSpec((tm, D_CONV), lambda m: (m, gcol)),
                  pl.BlockSpec((CONV_HALO, D_CONV), lambda m: (hrow(m), vcol)),
                  pl.BlockSpec((CONV_HALO, D_CONV), lambda m: (hrow(m), gcol)),
                  pl.BlockSpec((CONV_WIDTH, D_CONV), lambda m: (0, 0)),
                  vec, vec, vec],
        out_specs=pl.BlockSpec((tm, D_CONV), lambda m: (m, 0)),
        out_shape=jax.ShapeDtypeStruct((s, D_CONV), BF16),
        scratch_shapes=[pltpu.VMEM((CONV_HALO + tm, D_CONV), F32)],
        compiler_params=_cparams(("parallel",)),
    )(proj, proj, proj, proj, conv_w, conv_b.reshape(1, -1), ln_g.reshape(1, -1), ln_b.reshape(1, -1))


def _outproj_kernel(att_ref, cv_ref, w_ref, h_ref, g_ref, hnew_ref, hn_ref, wb_ref):
    @pl.when(pl.program_id(0) == 0)
    def _():
        wb_ref[...] = w_ref[...].astype(BF16)

    y = jnp.dot(att_ref[...], wb_ref[0:D_ATT, :], preferred_element_type=F32)
    y = y + jnp.dot(cv_ref[...], wb_ref[D_ATT:D_ATT + D_CONV, :], preferred_element_type=F32)
    hnew = h_ref[...] + y
    hnew_ref[...] = hnew
    hn_ref[...] = _rms(hnew, g_ref[...]).astype(hn_ref.dtype)


def outproj_residual_norm(att, cv, w_out, layer, h, g, tm=512):
    s, d = h.shape
    kin = w_out.shape[1]
    row = lambda m: (m, 0)
    fixed = lambda m: (0, 0)
    return pl.pallas_call(
        _outproj_kernel,
        grid=(s // tm,),
        in_specs=[pl.BlockSpec((tm, D_ATT), row),
                  pl.BlockSpec((tm, D_CONV), row),
                  pl.BlockSpec((None, kin, d), lambda m: (layer, 0, 0), pipeline_mode=pl.Buffered(1)),
                  pl.BlockSpec((tm, d), row),
                  pl.BlockSpec((1, d), fixed)],
        out_specs=[pl.BlockSpec((tm, d), row), pl.BlockSpec((tm, d), row)],
        out_shape=[jax.ShapeDtypeStruct((s, d), F32), jax.ShapeDtypeStruct((s, d), BF16)],
        scratch_shapes=[pltpu.VMEM((kin, d), BF16)],
        compiler_params=_cparams(("arbitrary",)),
    )(att, cv, w_out, h, g.reshape(1, d))


def _gluproj_kernel(y_ref, wa_ref, wg_ref, h_ref, g_ref, hnew_ref, hn_ref, wab_ref, wgb_ref):
    @pl.when(pl.program_id(0) == 0)
    def _():
        wab_ref[...] = wa_ref[...].astype(BF16)
        wgb_ref[...] = wg_ref[...].astype(BF16)

    y = y_ref[...].astype(BF16)
    a = jnp.dot(y, wab_ref[...], preferred_element_type=F32)
    gt = jnp.dot(y, wgb_ref[...], preferred_element_type=F32)
    hnew = h_ref[...] + a * jax.nn.sigmoid(gt)
    hnew_ref[...] = hnew
    hn_ref[...] = _rms(hnew, g_ref[...]).astype(hn_ref.dtype)


def gluproj_residual_norm(y, wa, wg, layer, h, g, tm=256):
    s, d = h.shape
    kin = wa.shape[1]
    row = lambda m: (m, 0)
    fixed = lambda m: (0, 0)
    once = lambda m: (layer, 0, 0)
    return pl.pallas_call(
        _gluproj_kernel,
        grid=(s // tm,),
        in_specs=[pl.BlockSpec((tm, kin), row),
                  pl.BlockSpec((None, kin, d), once, pipeline_mode=pl.Buffered(1)),
                  pl.BlockSpec((None, kin, d), once, pipeline_mode=pl.Buffered(1)),
                  pl.BlockSpec((tm, d), row),
                  pl.BlockSpec((1, d), fixed)],
        out_specs=[pl.BlockSpec((tm, d), row), pl.BlockSpec((tm, d), row)],
        out_shape=[jax.ShapeDtypeStruct((s, d), F32), jax.ShapeDtypeStruct((s, d), F32)],
        scratch_shapes=[pltpu.VMEM((kin, d), BF16), pltpu.VMEM((kin, d), BF16)],
        compiler_params=_cparams(("arbitrary",)),
    )(y, wa, wg, h, g.reshape(1, d))


def _swiglu_step(xb, w1_ref, w3_ref, w2_ref):
    a = jnp.dot(xb, w1_ref[...].astype(BF16), preferred_element_type=F32)
    b = jnp.dot(xb, w3_ref[...].astype(BF16), preferred_element_type=F32)
    p = (a * jax.nn.sigmoid(a) * b).astype(BF16)
    return jnp.dot(p, w2_ref[...].astype(BF16), preferred_element_type=F32)


def _ffn_kernel(x_ref, w1_ref, w3_ref, w2_ref, y_ref):
    @pl.when(pl.program_id(1) == 0)
    def _():
        y_ref[...] = jnp.zeros_like(y_ref)

    y_ref[...] += _swiglu_step(x_ref[...], w1_ref, w3_ref, w2_ref)


def ffn(x, w1, w3, w2, layer, tm=1024, tf=256):
    s, d = x.shape
    dff = w1.shape[2]
    row = lambda m, f: (m, 0)
    return pl.pallas_call(
        _ffn_kernel,
        grid=(s // tm, dff // tf),
        in_specs=[pl.BlockSpec((tm, d), row),
                  pl.BlockSpec((None, d, tf), lambda m, f: (layer, 0, f)),
                  pl.BlockSpec((None, d, tf), lambda m, f: (layer, 0, f)),
                  pl.BlockSpec((None, tf, d), lambda m, f: (layer, f, 0))],
        out_specs=pl.BlockSpec((tm, d), row),
        out_shape=jax.ShapeDtypeStruct((s, d), F32),
        compiler_params=_cparams(("parallel", "arbitrary")),
    )(x, w1, w3, w2)


def _ssm_tables(lam_re, lam_im, log_dt, b_re, b_im, c_re, c_im, d_skip, n_doublings):
    L = SSM_CHUNK
    G, P = lam_re.shape
    dt = jnp.exp(log_dt)[:, None]
    mag = jnp.exp(lam_re * dt)
    a_re = mag * jnp.cos(lam_im * dt)
    a_im = mag * jnp.sin(lam_im * dt)
    inv = 1.0 / (lam_re * lam_re + lam_im * lam_im)
    f_re = ((a_re - 1.0) * lam_re + a_im * lam_im) * inv
    f_im = (a_im * lam_re - (a_re - 1.0) * lam_im) * inv
    bb_re = f_re[..., None] * b_re - f_im[..., None] * b_im
    bb_im = f_re[..., None] * b_im + f_im[..., None] * b_re

    pw_re = [jnp.ones_like(a_re)]
    pw_im = [jnp.zeros_like(a_im)]
    for _ in range(L):
        pr, pi = pw_re[-1], pw_im[-1]
        pw_re.append(pr * a_re - pi * a_im)
        pw_im.append(pr * a_im + pi * a_re)
    pw_re = jnp.stack(pw_re)
    pw_im = jnp.stack(pw_im)

    ab_re = pw_re[:L, :, :, None] * bb_re[None] - pw_im[:L, :, :, None] * bb_im[None]
    ab_im = pw_re[:L, :, :, None] * bb_im[None] + pw_im[:L, :, :, None] * bb_re[None]
    hi = lax.Precision.HIGHEST
    kern = (jnp.einsum('gcp,tgpd->tgcd', c_re, ab_re, precision=hi)
            - jnp.einsum('gcp,tgpd->tgcd', c_im, ab_im, precision=hi))
    C = SSM_GROUP
    Q = SSM_LANES // C
    O = G // Q
    kern = kern.at[0].add(jnp.eye(C, dtype=F32)[None] * d_skip.reshape(G, C)[:, :, None])

    kc = kern.reshape(L, O, Q, C, C).transpose(1, 4, 0, 2, 3).reshape(O, C, L * Q * C)

    def by_state(t):
        return jnp.stack(t, axis=3).reshape(O, L, C, 2 * Q * P)

    bc = by_state([ab[::-1].reshape(L, O, Q, P, C).transpose(1, 0, 4, 2, 3) for ab in (ab_re, ab_im)])

    ca_re = c_re[None] * pw_re[1:, :, None, :] - c_im[None] * pw_im[1:, :, None, :]
    ca_im = c_re[None] * pw_im[1:, :, None, :] + c_im[None] * pw_re[1:, :, None, :]
    cc = by_state([ca.reshape(L, O, Q, C, P).transpose(1, 0, 3, 2, 4) for ca in (ca_re, -ca_im)])

    sq_re, sq_im = pw_re[L].reshape(O, Q * P), pw_im[L].reshape(O, Q * P)
    same, cross = [], []
    for _ in range(n_doublings):
        same.append(jnp.concatenate([sq_re, sq_re], axis=-1))
        cross.append(jnp.concatenate([-sq_im, sq_im], axis=-1))
        sq_re, sq_im = sq_re * sq_re - sq_im * sq_im, 2.0 * sq_re * sq_im
    return kc, bc, cc, jnp.stack(same, axis=1), jnp.stack(cross, axis=1)


def _ssm_kernel(u_ref, kc_ref, bc_ref, cc_ref, same_ref, cross_ref, y_ref, yacc, kcat):
    L = SSM_CHUNK
    nc = yacc.shape[0]
    n_state = same_ref.shape[1]
    half = n_state // 2
    groups = SSM_LANES // SSM_GROUP

    def spread(table, col_group):
        rows = lax.broadcasted_iota(jnp.int32, (SSM_LANES, table.shape[1]), 0)
        cols = lax.broadcasted_iota(jnp.int32, (SSM_LANES, table.shape[1]), 1)
        tiled = jnp.concatenate([table] * groups, axis=0)
        return jnp.where(rows // SSM_GROUP == col_group(cols), tiled, 0.0).astype(BF16)

    state_group = lambda cols: (cols % half) // SSM_STATE
    lane_group = lambda cols: (cols % SSM_LANES) // SSM_GROUP

    kcat[...] = spread(kc_ref[...], lane_group)
    yacc[...] = jnp.zeros_like(yacc)
    x = jnp.zeros((nc, n_state), F32)
    for j in range(L):
        uj = u_ref[pl.ds(j, nc, stride=L), :].astype(BF16)
        x = x + jnp.dot(uj, spread(bc_ref[j], state_group), preferred_element_type=F32)
        width = (L - j) * SSM_LANES
        yacc[:, j * SSM_LANES:] += jnp.dot(uj, kcat[:, :width], preferred_element_type=F32)
    row = lax.broadcasted_iota(jnp.int32, x.shape, 0)
    for k in range(same_ref.shape[0]):
        shift = 1 << k
        sh = jnp.where(row >= shift, pltpu.roll(x, shift, 0), 0.0)
        x = x + same_ref[k:k + 1, :] * sh + cross_ref[k:k + 1, :] * pltpu.roll(sh, half, 1)
    prev = jnp.where(row >= 1, pltpu.roll(x, 1, 0), 0.0).astype(BF16)
    for s in range(L):
        carried = lax.dot_general(prev, spread(cc_ref[s], state_group), (((1,), (1,)), ((), ())),
                                  preferred_element_type=F32)
        y_ref[pl.ds(s, nc, stride=L), :] = jax.nn.gelu(yacc[:, s * SSM_LANES:(s + 1) * SSM_LANES] + carried)


def ssm_core(u, params):
    s, width = u.shape
    L = SSM_CHUNK
    nc = s // L
    n_doublings = max(1, (nc - 1).bit_length())
    kc, bc, cc, same, cross = _ssm_tables(*params, n_doublings=n_doublings)
    n_blocks = width // SSM_LANES
    n_state = same.shape[2]
    blk3 = lambda o: (o, 0, 0)
    blk4 = lambda o: (o, 0, 0, 0)
    return pl.pallas_call(
        _ssm_kernel,
        grid=(n_blocks,),
        in_specs=[pl.BlockSpec((s, SSM_LANES), lambda o: (0, o)),
                  pl.BlockSpec((None, SSM_GROUP, L * SSM_LANES), blk3),
                  pl.BlockSpec((None, L, SSM_GROUP, n_state), blk4),
                  pl.BlockSpec((None, L, SSM_GROUP, n_state), blk4),
                  pl.BlockSpec((None, n_doublings, n_state), blk3),
                  pl.BlockSpec((None, n_doublings, n_state), blk3)],
        out_specs=pl.BlockSpec((s, SSM_LANES), lambda o: (0, o)),
        out_shape=jax.ShapeDtypeStruct((s, width), F32),
        scratch_shapes=[pltpu.VMEM((nc, L * SSM_LANES), F32),
                        pltpu.VMEM((SSM_LANES, L * SSM_LANES), BF16)],
        compiler_params=_cparams(("parallel",)),
    )(u, kc, bc, cc, same, cross)


def _router_kernel(x_ref, r_ref, o_ref):
    logits = jnp.dot(x_ref[...], r_ref[...], preferred_element_type=F32, precision=lax.Precision.HIGHEST)
    lane = lax.broadcasted_iota(jnp.int32, logits.shape, 1)
    logits = jnp.where(lane < N_EXPERTS, logits, -jnp.inf)
    big = jnp.int32(logits.shape[1])
    v1 = jnp.max(logits, axis=-1, keepdims=True)
    i1 = jnp.min(jnp.where(logits == v1, lane, big), axis=-1, keepdims=True)
    rest = jnp.where(lane == i1, -jnp.inf, logits)
    v2 = jnp.max(rest, axis=-1, keepdims=True)
    i2 = jnp.min(jnp.where(rest == v2, lane, big), axis=-1, keepdims=True)
    e2 = jnp.exp(v2 - v1)
    g1 = 1.0 / (1.0 + e2)
    g2 = e2 / (1.0 + e2)
    out = jnp.where(lane == 0, i1.astype(F32), 0.0)
    out = jnp.where(lane == 1, i2.astype(F32), out)
    out = jnp.where(lane == 2, g1, out)
    out = jnp.where(lane == 3, g2, out)
    o_ref[...] = out


def router_top2(x, router, tm=512):
    s, d = x.shape
    lanes = 128
    rp = jnp.zeros((d, lanes), F32).at[:, :N_EXPERTS].set(router)
    return pl.pallas_call(
        _router_kernel,
        grid=(s // tm,),
        in_specs=[pl.BlockSpec((tm, d), lambda m: (m, 0)),
                  pl.BlockSpec((d, lanes), lambda m: (0, 0))],
        out_specs=pl.BlockSpec((tm, lanes), lambda m: (m, 0)),
        out_shape=jax.ShapeDtypeStruct((s, lanes), F32),
        compiler_params=_cparams(("parallel",)),
    )(x, rp)


def _moe_kernel(texp_ref, trows_ref, perm_ref,
                x_hbm, w1_ref, w3_ref, w2_ref, y_hbm, xb, acc, gsem, ssem):
    t = pl.program_id(0)
    f = pl.program_id(1)
    tm = acc.shape[0]
    n_tok = x_hbm.shape[0]
    rows = trows_ref[t]
    base = t * tm

    def gather_copy(i):
        a = perm_ref[base + i]
        tok = jnp.where(a >= n_tok, a - n_tok, a)
        return pltpu.make_async_copy(x_hbm.at[pl.ds(tok, 1), :], acc.at[pl.ds(i, 1), :], gsem)

    def scatter_copy(i):
        return pltpu.make_async_copy(acc.at[pl.ds(i, 1), :], y_hbm.at[pl.ds(perm_ref[base + i], 1), :], ssem)

    def for_rows(fn):
        groups = rows // DMA_UNROLL

        def group_body(gi, carry):
            for u in range(DMA_UNROLL):
                fn(gi * DMA_UNROLL + u)
            return carry

        def row_body(i, carry):
            fn(i)
            return carry

        lax.fori_loop(0, groups, group_body, 0)
        lax.fori_loop(groups * DMA_UNROLL, rows, row_body, 0)

    def start_all(make):
        for_rows(lambda i: make(i).start())

    def wait_all(make):
        for_rows(lambda i: make(i).wait())

    @pl.when(rows > 0)
    def _():
        @pl.when(f == 0)
        def _():
            acc[...] = jnp.zeros_like(acc)
            start_all(gather_copy)
            wait_all(gather_copy)
            xb[...] = acc[...].astype(BF16)
            acc[...] = jnp.zeros_like(acc)

        lo = 0
        for m in _moe_row_buckets(tm):
            @pl.when((rows > lo) & (rows <= m))
            def _(m=m):
                acc[0:m, :] += _swiglu_step(xb[0:m, :], w1_ref, w3_ref, w2_ref)
            lo = m

        @pl.when(f == pl.num_programs(1) - 1)
        def _():
            start_all(scatter_copy)
            wait_all(scatter_copy)


def _moe_row_buckets(tm):
    return tuple(range(2 * MOE_ROW_STEP, tm, MOE_ROW_STEP)) + (tm,)


def moe_experts(x, w1, w3, w2, layer, tile_expert, tile_rows, perm, tm, tf=256):
    s, d = x.shape
    dff = w1.shape[3]
    n_tiles = tile_expert.shape[0]
    nf = dff // tf
    last = nf - 1

    def fsel(t, f, trows):
        return jnp.where(trows[t] > 0, f, last)

    grid_spec = pltpu.PrefetchScalarGridSpec(
        num_scalar_prefetch=3,
        grid=(n_tiles, nf),
        in_specs=[pl.BlockSpec(memory_space=pl.ANY),
                  pl.BlockSpec((None, None, d, tf), lambda t, f, te, tr, pm: (layer, te[t], 0, fsel(t, f, tr))),
                  pl.BlockSpec((None, None, d, tf), lambda t, f, te, tr, pm: (layer, te[t], 0, fsel(t, f, tr))),
                  pl.BlockSpec((None, None, tf, d), lambda t, f, te, tr, pm: (layer, te[t], fsel(t, f, tr), 0))],
        out_specs=pl.BlockSpec(memory_space=pl.ANY),
        scratch_shapes=[pltpu.VMEM((tm, d), BF16),
                        pltpu.VMEM((tm, d), F32),
                        pltpu.SemaphoreType.DMA(()),
                        pltpu.SemaphoreType.DMA(())],
    )
    return pl.pallas_call(
        _moe_kernel,
        grid_spec=grid_spec,
        out_shape=jax.ShapeDtypeStruct((TOP_K * s, d), F32),
        compiler_params=_cparams(("arbitrary", "arbitrary")),
    )(tile_expert, tile_rows, perm, x, w1, w3, w2)


def _route_plan(e1, e2, tm, n_tiles):
    s = e1.shape[0]
    experts = jnp.concatenate([e1, e2])
    onehot = (experts[:, None] == jnp.arange(N_EXPERTS, dtype=jnp.int32)[None, :]).astype(jnp.int32)
    csum = jnp.cumsum(onehot, axis=0)
    counts = csum[-1]
    tiles_per = (counts + tm - 1) // tm
    tile_end = jnp.cumsum(tiles_per)
    tile_start = tile_end - tiles_per
    per_tile = (counts + jnp.maximum(tiles_per, 1) - 1) // jnp.maximum(tiles_per, 1)
    rank = jnp.sum(onehot * (csum - 1), axis=1)
    own_start = jnp.sum(onehot * tile_start[None, :], axis=1)
    own_per_tile = jnp.maximum(jnp.sum(onehot * per_tile[None, :], axis=1), 1)
    pos = (own_start + rank // own_per_tile) * tm + rank % own_per_tile
    n_rows = n_tiles * tm
    perm = jnp.zeros((n_rows,), jnp.int32).at[pos].set(jnp.arange(TOP_K * s, dtype=jnp.int32))
    tid = jnp.arange(n_tiles, dtype=jnp.int32)
    texp = jnp.minimum(jnp.sum((tid[:, None] >= tile_end[None, :]).astype(jnp.int32), axis=1), N_EXPERTS - 1)
    used = tid < tile_end[-1]
    local = tid - tile_start[texp]
    trows = jnp.where(used, jnp.clip(counts[texp] - local * per_tile[texp], 0, per_tile[texp]), 0).astype(jnp.int32)
    last_used = texp[jnp.maximum(tile_end[-1] - 1, 0)]
    texp = jnp.where(used, texp, last_used).astype(jnp.int32)
    return texp, trows, perm


def _combine_kernel(h_ref, y1_ref, y2_ref, gt_ref, g_ref, *out_refs, want_h):
    gates = gt_ref[...]
    hnew = h_ref[...] + gates[:, 2:3] * y1_ref[...] + gates[:, 3:4] * y2_ref[...]
    if want_h:
        out_refs[0][...] = hnew
    out_refs[-1][...] = _rms(hnew, g_ref[...]).astype(out_refs[-1].dtype)


def combine_residual_norm(h, y, route, g, want_h, norm_dtype, tm=512):
    s, d = h.shape
    nb = s // tm
    row = lambda m: (m, 0)
    outs_spec = [pl.BlockSpec((tm, d), row)]
    outs_shape = [jax.ShapeDtypeStruct((s, d), norm_dtype)]
    if want_h:
        outs_spec = [pl.BlockSpec((tm, d), row)] + outs_spec
        outs_shape = [jax.ShapeDtypeStruct((s, d), F32)] + outs_shape
    return pl.pallas_call(
        functools.partial(_combine_kernel, want_h=want_h),
        grid=(nb,),
        in_specs=[pl.BlockSpec((tm, d), row),
                  pl.BlockSpec((tm, d), row),
                  pl.BlockSpec((tm, d), lambda m: (m + nb, 0)),
                  pl.BlockSpec((tm, route.shape[1]), row),
                  pl.BlockSpec((1, d), lambda m: (0, 0))],
        out_specs=outs_spec,
        out_shape=outs_shape,
        compiler_params=_cparams(("parallel",)),
    )(h, y, y, route, g.reshape(1, d))


MOE_TM = 2176


def kernel(x, even_norm1, even_w_in, even_conv_w, even_conv_b, even_cln_g, even_cln_b, even_w_out, even_norm2, even_ffn_w1, even_ffn_w3, even_ffn_w2, odd_norm1, odd_ssm_w_in, odd_lam_re, odd_lam_im, odd_log_dt, odd_b_re, odd_b_im, odd_c_re, odd_c_im, odd_d_skip, odd_glu_wa, odd_glu_wg, odd_norm2, odd_router, odd_moe_w1, odd_moe_w3, odd_moe_w2, final_norm):
    b, s, d = x.shape
    assert b == 1
    depth = even_norm1.shape[0] + odd_norm1.shape[0]
    assert even_norm1.shape[0] == odd_norm1.shape[0]
    h = x.reshape(s, d)
    hn = rmsnorm(h, even_norm1[0], BF16)
    n_tiles = (TOP_K * s) // MOE_TM + N_EXPERTS
    out = None
    ffn_out = None
    for layer in range(depth):
        j = layer // 2
        if layer % 2 == 0:
            proj = matmul(hn, even_w_in, j)
            att = dilated_attention(proj)
            cv = conv_module(proj, even_conv_w[j], even_conv_b[j], even_cln_g[j], even_cln_b[j])
            h, hn = outproj_residual_norm(att, cv, even_w_out, j, h, even_norm2[j])
            ffn_out = ffn(hn, even_ffn_w1, even_ffn_w3, even_ffn_w2, j)
        else:
            h, u = residual_norm_matmul(h, ffn_out, odd_norm1[j], odd_ssm_w_in, j)
            y = ssm_core(u, (odd_lam_re[j], odd_lam_im[j], odd_log_dt[j], odd_b_re[j], odd_b_im[j],
                             odd_c_re[j], odd_c_im[j], odd_d_skip[j]))
            h, hn = gluproj_residual_norm(y, odd_glu_wa, odd_glu_wg, j, h, odd_norm2[j])
            route = router_top2(hn, odd_router[j])
            e1 = route[:, 0].astype(jnp.int32)
            e2 = route[:, 1].astype(jnp.int32)
            texp, trows, perm = _route_plan(e1, e2, MOE_TM, n_tiles)
            ys = moe_experts(hn, odd_moe_w1, odd_moe_w3, odd_moe_w2, j, texp, trows, perm, MOE_TM)
            if layer + 1 < depth:
                h, hn = combine_residual_norm(h, ys, route, even_norm1[j + 1], True, BF16)
            else:
                (out,) = combine_residual_norm(h, ys, route, final_norm, False, F32)
    return out.reshape(b, s, d)
```

```python
import functools
import math

import jax
import jax.numpy as jnp
from jax import lax
from jax.experimental import pallas as pl
from jax.experimental.pallas import tpu as pltpu

F32 = jnp.float32
BF16 = jnp.bfloat16

NORM_EPS = 1e-5
N_HEADS = 8
HEAD_DIM = 128
D_ATT = N_HEADS * HEAD_DIM
DILATIONS = (1, 4, 16)
BAND = 128
ATT_CHUNK = BAND * DILATIONS[-1]
ATT_UNROLL = {1: 16, 4: 16, 16: 8}
ALIBI_MAX_BIAS = 8.0
D_CONV = 1024
CONV_WIDTH = 31
CONV_HALO = 32
SSM_GROUP = 16
SSM_STATE = 64
SSM_CHUNK = 16
SSM_LANES = 128
N_EXPERTS = 8
TOP_K = 2
DMA_UNROLL = 8
MOE_ROW_STEP = 256

VMEM_LIMIT = 56 * 1024 * 1024


def _cparams(sem):
    return pltpu.CompilerParams(dimension_semantics=sem, vmem_limit_bytes=VMEM_LIMIT)


def _rms(x, g):
    return x * lax.rsqrt(jnp.mean(x * x, axis=-1, keepdims=True) + NORM_EPS) * g


def _rmsnorm_kernel(x_ref, g_ref, o_ref):
    o_ref[...] = _rms(x_ref[...], g_ref[...]).astype(o_ref.dtype)


def rmsnorm(x, g, out_dtype, tm=512):
    s, d = x.shape
    return pl.pallas_call(
        _rmsnorm_kernel,
        grid=(s // tm,),
        in_specs=[pl.BlockSpec((tm, d), lambda m: (m, 0)),
                  pl.BlockSpec((1, d), lambda m: (0, 0))],
        out_specs=pl.BlockSpec((tm, d), lambda m: (m, 0)),
        out_shape=jax.ShapeDtypeStruct((s, d), out_dtype),
        compiler_params=_cparams(("parallel",)),
    )(x, g.reshape(1, d))


def _matmul_kernel(x_ref, w_ref, o_ref, wb_ref):
    @pl.when(pl.program_id(1) == 0)
    def _():
        wb_ref[...] = w_ref[...].astype(BF16)

    o_ref[...] = jnp.dot(x_ref[...], wb_ref[...], preferred_element_type=F32).astype(o_ref.dtype)


def matmul(x, w, layer, out_dtype=F32, tm=2048, tn=512):
    s, k = x.shape
    n = w.shape[2]
    tn = min(tn, n)
    return pl.pallas_call(
        _matmul_kernel,
        grid=(n // tn, s // tm),
        in_specs=[pl.BlockSpec((tm, k), lambda j, m: (m, 0)),
                  pl.BlockSpec((None, k, tn), lambda j, m: (layer, 0, j))],
        out_specs=pl.BlockSpec((tm, tn), lambda j, m: (m, j)),
        out_shape=jax.ShapeDtypeStruct((s, n), out_dtype),
        scratch_shapes=[pltpu.VMEM((k, tn), BF16)],
        compiler_params=_cparams(("arbitrary", "arbitrary")),
    )(x, w)


def _resnorm_matmul_kernel(h_ref, y_ref, g_ref, w_ref, hnew_ref, o_ref, wb_ref):
    @pl.when(pl.program_id(0) == 0)
    def _():
        wb_ref[...] = w_ref[...].astype(BF16)

    hnew = h_ref[...] + y_ref[...]
    hnew_ref[...] = hnew
    xn = _rms(hnew, g_ref[...]).astype(BF16)
    o_ref[...] = jnp.dot(xn, wb_ref[...], preferred_element_type=F32).astype(o_ref.dtype)


def residual_norm_matmul(h, y, g, w, layer, tm=512):
    s, d = h.shape
    n = w.shape[2]
    row = lambda m: (m, 0)
    return pl.pallas_call(
        _resnorm_matmul_kernel,
        grid=(s // tm,),
        in_specs=[pl.BlockSpec((tm, d), row),
                  pl.BlockSpec((tm, d), row),
                  pl.BlockSpec((1, d), lambda m: (0, 0)),
                  pl.BlockSpec((None, d, n), lambda m: (layer, 0, 0), pipeline_mode=pl.Buffered(1))],
        out_specs=[pl.BlockSpec((tm, d), row), pl.BlockSpec((tm, n), row)],
        out_shape=[jax.ShapeDtypeStruct((s, d), F32), jax.ShapeDtypeStruct((s, n), F32)],
        scratch_shapes=[pltpu.VMEM((d, n), BF16)],
        compiler_params=_cparams(("arbitrary",)),
    )(h, y, g.reshape(1, d), w)


def _attn_kernel(q_ref, kp_ref, kc_ref, vp_ref, vc_ref, o_ref, kk, vv, osc, lsc):
    h = pl.program_id(0)
    c = pl.program_id(1)
    kk[0:ATT_CHUNK, :] = kp_ref[...]
    kk[ATT_CHUNK:2 * ATT_CHUNK, :] = kc_ref[...]
    vv[0:ATT_CHUNK, :] = vp_ref[...]
    vv[ATT_CHUNK:2 * ATT_CHUNK, :] = vc_ref[...]

    qi = lax.broadcasted_iota(jnp.int32, (BAND, 2 * BAND), 0)
    kj = lax.broadcasted_iota(jnp.int32, (BAND, 2 * BAND), 1)
    dist = BAND + qi - kj
    band = (dist >= 0) & (dist <= BAND)
    head = jnp.full((BAND, 2 * BAND), h + 1, jnp.int32).astype(F32)
    slope = jnp.exp2(-(ALIBI_MAX_BIAS / N_HEADS) * head)
    sdist = slope * dist.astype(F32)
    scale = HEAD_DIM ** -0.5
    blocks = ATT_CHUNK // BAND

    for g, d in enumerate(DILATIONS):
        bias = jnp.where(band, -sdist * float(d), -jnp.inf)
        bias_first = jnp.where(kj >= BAND, bias, -jnp.inf)

        def one_block(b, g=g, d=d, bias=bias, bias_first=bias_first):
            n = b // d
            r = b % d
            qstart = n * (BAND * d) + r
            kstart = ATT_CHUNK + (n - 1) * (BAND * d) + r
            q = q_ref[pl.ds(qstart, BAND, stride=d), :].astype(BF16)
            k = kk[pl.ds(kstart, 2 * BAND, stride=d), :].astype(BF16)
            v = vv[pl.ds(kstart, 2 * BAND, stride=d), :].astype(BF16)
            s = lax.dot_general(q, k, (((1,), (1,)), ((), ())), preferred_element_type=F32) * scale
            s = s + jnp.where((c == 0) & (n == 0), bias_first, bias)
            m = jnp.max(s, axis=-1, keepdims=True)
            p = jnp.exp(s - m)
            den = jnp.sum(p, axis=-1, keepdims=True)
            o = jnp.dot(p.astype(BF16), v, preferred_element_type=F32) / den
            lse = m + jnp.log(den)
            osc[g, pl.ds(qstart, BAND, stride=d), :] = o
            lsc[g, pl.ds(qstart, BAND, stride=d), :] = jnp.broadcast_to(lse, (BAND, HEAD_DIM))

        unroll = ATT_UNROLL[d]

        def body(i, carry, one_block=one_block, unroll=unroll):
            for u in range(unroll):
                one_block(i * unroll + u)
            return carry

        lax.fori_loop(0, blocks // unroll, body, 0)

    l0, l1, l2 = lsc[0], lsc[1], lsc[2]
    m = jnp.maximum(jnp.maximum(l0, l1), l2)
    w0, w1, w2 = jnp.exp(l0 - m), jnp.exp(l1 - m), jnp.exp(l2 - m)
    att = (w0 * osc[0] + w1 * osc[1] + w2 * osc[2]) / (w0 + w1 + w2)
    o_ref[...] = att.astype(o_ref.dtype)


def dilated_attention(proj):
    s = proj.shape[0]
    blk = (ATT_CHUNK, HEAD_DIM)
    prev = lambda c: jnp.maximum(c - 1, 0)
    return pl.pallas_call(
        _attn_kernel,
        grid=(N_HEADS, s // ATT_CHUNK),
        in_specs=[pl.BlockSpec(blk, lambda h, c: (c, h)),
                  pl.BlockSpec(blk, lambda h, c: (prev(c), N_HEADS + h)),
                  pl.BlockSpec(blk, lambda h, c: (c, N_HEADS + h)),
                  pl.BlockSpec(blk, lambda h, c: (prev(c), 2 * N_HEADS + h)),
                  pl.BlockSpec(blk, lambda h, c: (c, 2 * N_HEADS + h))],
        out_specs=pl.BlockSpec(blk, lambda h, c: (c, h)),
        out_shape=jax.ShapeDtypeStruct((s, D_ATT), BF16),
        scratch_shapes=[pltpu.VMEM((2 * ATT_CHUNK, HEAD_DIM), F32),
                        pltpu.VMEM((2 * ATT_CHUNK, HEAD_DIM), F32),
                        pltpu.VMEM((len(DILATIONS), ATT_CHUNK, HEAD_DIM), F32),
                        pltpu.VMEM((len(DILATIONS), ATT_CHUNK, HEAD_DIM), F32)],
        compiler_params=_cparams(("parallel", "parallel")),
    )(proj, proj, proj, proj, proj)


def _conv_kernel(val_ref, gate_ref, hval_ref, hgate_ref, w_ref, b_ref, lg_ref, lb_ref, o_ref, glu):
    m = pl.program_id(0)
    tm = val_ref.shape[0]
    halo = hval_ref[...] * jax.nn.sigmoid(hgate_ref[...])
    glu[0:CONV_HALO, :] = jnp.where(m > 0, halo, 0.0)
    glu[CONV_HALO:CONV_HALO + tm, :] = val_ref[...] * jax.nn.sigmoid(gate_ref[...])
    first = CONV_HALO - (CONV_WIDTH - 1)

    acc = jnp.broadcast_to(b_ref[...], (tm, D_CONV))
    for j in range(CONV_WIDTH):
        acc = acc + w_ref[j:j + 1, :] * glu[pl.ds(first + j, tm), :]
    mu = jnp.mean(acc, axis=-1, keepdims=True)
    cen = acc - mu
    var = jnp.mean(cen * cen, axis=-1, keepdims=True)
    y = cen * lax.rsqrt(var + NORM_EPS) * lg_ref[...] + lb_ref[...]
    o_ref[...] = (y * jax.nn.sigmoid(y)).astype(o_ref.dtype)


def conv_module(proj, conv_w, conv_b, ln_g, ln_b, tm=512):
    s = proj.shape[0]
    vcol = 3 * D_ATT // D_CONV
    gcol = vcol + 1
    per = tm // CONV_HALO
    hrow = lambda m: jnp.maximum(m * per - 1, 0)
    vec = pl.BlockSpec((1, D_CONV), lambda m: (0, 0))
    return pl.pallas_call(
        _conv_kernel,
        grid=(s // tm,),
        in_specs=[pl.BlockSpec((tm, D_CONV), lambda m: (m, vcol)),
                  pl.BlockSpec((tm, D_CONV), lambda m: (m, gcol)),
                  pl.BlockSpec((CONV_HALO, D_CONV), lambda m: (hrow(m), vcol)),
                  pl.BlockSpec((CONV_HALO, D_CONV), lambda m: (hrow(m), gcol)),
                  pl.BlockSpec((CONV_WIDTH, D_CONV), lambda m: (0, 0)),
                  vec, vec, vec],
        out_specs=pl.BlockSpec((tm, D_CONV), lambda m: (m, 0)),
        out_shape=jax.ShapeDtypeStruct((s, D_CONV), BF16),
        scratch_shapes=[pltpu.VMEM((CONV_HALO + tm, D_CONV), F32)],
        compiler_params=_cparams(("parallel",)),
    )(proj, proj, proj, proj, conv_w, conv_b.reshape(1, -1), ln_g.reshape(1, -1), ln_b.reshape(1, -1))


def _outproj_kernel(att_ref, cv_ref, w_ref, h_ref, g_ref, hnew_ref, hn_ref, wb_ref):
    @pl.when(pl.program_id(0) == 0)
    def _():
        wb_ref[...] = w_ref[...].astype(BF16)

    y = jnp.dot(att_ref[...], wb_ref[0:D_ATT, :], preferred_element_type=F32)
    y = y + jnp.dot(cv_ref[...], wb_ref[D_ATT:D_ATT + D_CONV, :], preferred_element_type=F32)
    hnew = h_ref[...] + y
    hnew_ref[...] = hnew
    hn_ref[...] = _rms(hnew, g_ref[...]).astype(hn_ref.dtype)


def outproj_residual_norm(att, cv, w_out, layer, h, g, tm=512):
    s, d = h.shape
    kin = w_out.shape[1]
    row = lambda m: (m, 0)
    fixed = lambda m: (0, 0)
    return pl.pallas_call(
        _outproj_kernel,
        grid=(s // tm,),
        in_specs=[pl.BlockSpec((tm, D_ATT), row),
                  pl.BlockSpec((tm, D_CONV), row),
                  pl.BlockSpec((None, kin, d), lambda m: (layer, 0, 0), pipeline_mode=pl.Buffered(1)),
                  pl.BlockSpec((tm, d), row),
                  pl.BlockSpec((1, d), fixed)],
        out_specs=[pl.BlockSpec((tm, d), row), pl.BlockSpec((tm, d), row)],
        out_shape=[jax.ShapeDtypeStruct((s, d), F32), jax.ShapeDtypeStruct((s, d), BF16)],
        scratch_shapes=[pltpu.VMEM((kin, d), BF16)],
        compiler_params=_cparams(("arbitrary",)),
    )(att, cv, w_out, h, g.reshape(1, d))


def _gluproj_kernel(y_ref, wa_ref, wg_ref, h_ref, g_ref, hnew_ref, hn_ref, wab_ref, wgb_ref):
    @pl.when(pl.program_id(0) == 0)
    def _():
        wab_ref[...] = wa_ref[...].astype(BF16)
        wgb_ref[...] = wg_ref[...].astype(BF16)

    y = y_ref[...].astype(BF16)
    a = jnp.dot(y, wab_ref[...], preferred_element_type=F32)
    gt = jnp.dot(y, wgb_ref[...], preferred_element_type=F32)
    hnew = h_ref[...] + a * jax.nn.sigmoid(gt)
    hnew_ref[...] = hnew
    hn_ref[...] = _rms(hnew, g_ref[...]).astype(hn_ref.dtype)


def gluproj_residual_norm(y, wa, wg, layer, h, g, tm=256):
    s, d = h.shape
    kin = wa.shape[1]
    row = lambda m: (m, 0)
    fixed = lambda m: (0, 0)
    once = lambda m: (layer, 0, 0)
    return pl.pallas_call(
        _gluproj_kernel,
        grid=(s // tm,),
        in_specs=[pl.BlockSpec((tm, kin), row),
                  pl.BlockSpec((None, kin, d), once, pipeline_mode=pl.Buffered(1)),
                  pl.BlockSpec((None, kin, d), once, pipeline_mode=pl.Buffered(1)),
                  pl.BlockSpec((tm, d), row),
                  pl.BlockSpec((1, d), fixed)],
        out_specs=[pl.BlockSpec((tm, d), row), pl.BlockSpec((tm, d), row)],
        out_shape=[jax.ShapeDtypeStruct((s, d), F32), jax.ShapeDtypeStruct((s, d), F32)],
        scratch_shapes=[pltpu.VMEM((kin, d), BF16), pltpu.VMEM((kin, d), BF16)],
        compiler_params=_cparams(("arbitrary",)),
    )(y, wa, wg, h, g.reshape(1, d))


def _swiglu_step(xb, w1_ref, w3_ref, w2_ref):
    a = jnp.dot(xb, w1_ref[...].astype(BF16), preferred_element_type=F32)
    b = jnp.dot(xb, w3_ref[...].astype(BF16), preferred_element_type=F32)
    p = (a * jax.nn.sigmoid(a) * b).astype(BF16)
    return jnp.dot(p, w2_ref[...].astype(BF16), preferred_element_type=F32)


def _ffn_kernel(x_ref, w1_ref, w3_ref, w2_ref, y_ref):
    @pl.when(pl.program_id(1) == 0)
    def _():
        y_ref[...] = jnp.zeros_like(y_ref)

    y_ref[...] += _swiglu_step(x_ref[...], w1_ref, w3_ref, w2_ref)


def ffn(x, w1, w3, w2, layer, tm=1024, tf=256):
    s, d = x.shape
    dff = w1.shape[2]
    row = lambda m, f: (m, 0)
    return pl.pallas_call(
        _ffn_kernel,
        grid=(s // tm, dff // tf),
        in_specs=[pl.BlockSpec((tm, d), row),
                  pl.BlockSpec((None, d, tf), lambda m, f: (layer, 0, f)),
                  pl.BlockSpec((None, d, tf), lambda m, f: (layer, 0, f)),
                  pl.BlockSpec((None, tf, d), lambda m, f: (layer, f, 0))],
        out_specs=pl.BlockSpec((tm, d), row),
        out_shape=jax.ShapeDtypeStruct((s, d), F32),
        compiler_params=_cparams(("parallel", "arbitrary")),
    )(x, w1, w3, w2)


def _ssm_tables(lam_re, lam_im, log_dt, b_re, b_im, c_re, c_im, d_skip, n_doublings):
    L = SSM_CHUNK
    G, P = lam_re.shape
    dt = jnp.exp(log_dt)[:, None]
    mag = jnp.exp(lam_re * dt)
    a_re = mag * jnp.cos(lam_im * dt)
    a_im = mag * jnp.sin(lam_im * dt)
    inv = 1.0 / (lam_re * lam_re + lam_im * lam_im)
    f_re = ((a_re - 1.0) * lam_re + a_im * lam_im) * inv
    f_im = (a_im * lam_re - (a_re - 1.0) * lam_im) * inv
    bb_re = f_re[..., None] * b_re - f_im[..., None] * b_im
    bb_im = f_re[..., None] * b_im + f_im[..., None] * b_re

    pw_re = [jnp.ones_like(a_re)]
    pw_im = [jnp.zeros_like(a_im)]
    for _ in range(L):
        pr, pi = pw_re[-1], pw_im[-1]
        pw_re.append(pr * a_re - pi * a_im)
        pw_im.append(pr * a_im + pi * a_re)
    pw_re = jnp.stack(pw_re)
    pw_im = jnp.stack(pw_im)

    ab_re = pw_re[:L, :, :, None] * bb_re[None] - pw_im[:L, :, :, None] * bb_im[None]
    ab_im = pw_re[:L, :, :, None] * bb_im[None] + pw_im[:L, :, :, None] * bb_re[None]
    hi = lax.Precision.HIGHEST
    kern = (jnp.einsum('gcp,tgpd->tgcd', c_re, ab_re, precision=hi)
            - jnp.einsum('gcp,tgpd->tgcd', c_im, ab_im, precision=hi))
    C = SSM_GROUP
    Q = SSM_LANES // C
    O = G // Q
    kern = kern.at[0].add(jnp.eye(C, dtype=F32)[None] * d_skip.reshape(G, C)[:, :, None])

    kc = kern.reshape(L, O, Q, C, C).transpose(1, 4, 0, 2, 3).reshape(O, C, L * Q * C)

    def by_state(t):
        return jnp.stack(t, axis=3).reshape(O, L, C, 2 * Q * P)

    bc = by_state([ab[::-1].reshape(L, O, Q, P, C).transpose(1, 0, 4, 2, 3) for ab in (ab_re, ab_im)])

    ca_re = c_re[None] * pw_re[1:, :, None, :] - c_im[None] * pw_im[1:, :, None, :]
    ca_im = c_re[None] * pw_im[1:, :, None, :] + c_im[None] * pw_re[1:, :, None, :]
    cc = by_state([ca.reshape(L, O, Q, C, P).transpose(1, 0, 3, 2, 4) for ca in (ca_re, -ca_im)])

    sq_re, sq_im = pw_re[L].reshape(O, Q * P), pw_im[L].reshape(O, Q * P)
    same, cross = [], []
    for _ in range(n_doublings):
        same.append(jnp.concatenate([sq_re, sq_re], axis=-1))
        cross.append(jnp.concatenate([-sq_im, sq_im], axis=-1))
        sq_re, sq_im = sq_re * sq_re - sq_im * sq_im, 2.0 * sq_re * sq_im
    return kc, bc, cc, jnp.stack(same, axis=1), jnp.stack(cross, axis=1)


def _ssm_kernel(u_ref, kc_ref, bc_ref, cc_ref, same_ref, cross_ref, y_ref, yacc, kcat):
    L = SSM_CHUNK
    nc = yacc.shape[0]
    n_state = same_ref.shape[1]
    half = n_state // 2
    groups = SSM_LANES // SSM_GROUP

    def spread(table, col_group):
        rows = lax.broadcasted_iota(jnp.int32, (SSM_LANES, table.shape[1]), 0)
        cols = lax.broadcasted_iota(jnp.int32, (SSM_LANES, table.shape[1]), 1)
        tiled = jnp.concatenate([table] * groups, axis=0)
        return jnp.where(rows // SSM_GROUP == col_group(cols), tiled, 0.0).astype(BF16)

    state_group = lambda cols: (cols % half) // SSM_STATE
    lane_group = lambda cols: (cols % SSM_LANES) // SSM_GROUP

    kcat[...] = spread(kc_ref[...], lane_group)
    yacc[...] = jnp.zeros_like(yacc)
    x = jnp.zeros((nc, n_state), F32)
    for j in range(L):
        uj = u_ref[pl.ds(j, nc, stride=L), :].astype(BF16)
        x = x + jnp.dot(uj, spread(bc_ref[j], state_group), preferred_element_type=F32)
        width = (L - j) * SSM_LANES
        yacc[:, j * SSM_LANES:] += jnp.dot(uj, kcat[:, :width], preferred_element_type=F32)
    row = lax.broadcasted_iota(jnp.int32, x.shape, 0)
    for k in range(same_ref.shape[0]):
        shift = 1 << k
        sh = jnp.where(row >= shift, pltpu.roll(x, shift, 0), 0.0)
        x = x + same_ref[k:k + 1, :] * sh + cross_ref[k:k + 1, :] * pltpu.roll(sh, half, 1)
    prev = jnp.where(row >= 1, pltpu.roll(x, 1, 0), 0.0).astype(BF16)
    for s in range(L):
        carried = lax.dot_general(prev, spread(cc_ref[s], state_group), (((1,), (1,)), ((), ())),
                                  preferred_element_type=F32)
        y_ref[pl.ds(s, nc, stride=L), :] = jax.nn.gelu(yacc[:, s * SSM_LANES:(s + 1) * SSM_LANES] + carried)


def ssm_core(u, params):
    s, width = u.shape
    L = SSM_CHUNK
    nc = s // L
    n_doublings = max(1, (nc - 1).bit_length())
    kc, bc, cc, same, cross = _ssm_tables(*params, n_doublings=n_doublings)
    n_blocks = width // SSM_LANES
    n_state = same.shape[2]
    blk3 = lambda o: (o, 0, 0)
    blk4 = lambda o: (o, 0, 0, 0)
    return pl.pallas_call(
        _ssm_kernel,
        grid=(n_blocks,),
        in_specs=[pl.BlockSpec((s, SSM_LANES), lambda o: (0, o)),
                  pl.BlockSpec((None, SSM_GROUP, L * SSM_LANES), blk3),
                  pl.BlockSpec((None, L, SSM_GROUP, n_state), blk4),
                  pl.BlockSpec((None, L, SSM_GROUP, n_state), blk4),
                  pl.BlockSpec((None, n_doublings, n_state), blk3),
                  pl.BlockSpec((None, n_doublings, n_state), blk3)],
        out_specs=pl.BlockSpec((s, SSM_LANES), lambda o: (0, o)),
        out_shape=jax.ShapeDtypeStruct((s, width), F32),
        scratch_shapes=[pltpu.VMEM((nc, L * SSM_LANES), F32),
                        pltpu.VMEM((SSM_LANES, L * SSM_LANES), BF16)],
        compiler_params=_cparams(("parallel",)),
    )(u, kc, bc, cc, same, cross)


def _router_kernel(x_ref, r_ref, o_ref):
    logits = jnp.dot(x_ref[...], r_ref[...], preferred_element_type=F32, precision=lax.Precision.HIGHEST)
    lane = lax.broadcasted_iota(jnp.int32, logits.shape, 1)
    logits = jnp.where(lane < N_EXPERTS, logits, -jnp.inf)
    big = jnp.int32(logits.shape[1])
    v1 = jnp.max(logits, axis=-1, keepdims=True)
    i1 = jnp.min(jnp.where(logits == v1, lane, big), axis=-1, keepdims=True)
    rest = jnp.where(lane == i1, -jnp.inf, logits)
    v2 = jnp.max(rest, axis=-1, keepdims=True)
    i2 = jnp.min(jnp.where(rest == v2, lane, big), axis=-1, keepdims=True)
    e2 = jnp.exp(v2 - v1)
    g1 = 1.0 / (1.0 + e2)
    g2 = e2 / (1.0 + e2)
    out = jnp.where(lane == 0, i1.astype(F32), 0.0)
    out = jnp.where(lane == 1, i2.astype(F32), out)
    out = jnp.where(lane == 2, g1, out)
    out = jnp.where(lane == 3, g2, out)
    o_ref[...] = out


def router_top2(x, router, tm=512):
    s, d = x.shape
    lanes = 128
    rp = jnp.zeros((d, lanes), F32).at[:, :N_EXPERTS].set(router)
    return pl.pallas_call(
        _router_kernel,
        grid=(s // tm,),
        in_specs=[pl.BlockSpec((tm, d), lambda m: (m, 0)),
                  pl.BlockSpec((d, lanes), lambda m: (0, 0))],
        out_specs=pl.BlockSpec((tm, lanes), lambda m: (m, 0)),
        out_shape=jax.ShapeDtypeStruct((s, lanes), F32),
        compiler_params=_cparams(("parallel",)),
    )(x, rp)


def _moe_kernel(texp_ref, trows_ref, perm_ref,
                x_hbm, w1_ref, w3_ref, w2_ref, y_hbm, xb, acc, gsem, ssem):
    t = pl.program_id(0)
    f = pl.program_id(1)
    tm = acc.shape[0]
    n_tok = x_hbm.shape[0]
    rows = trows_ref[t]
    base = t * tm

    def gather_copy(i):
        a = perm_ref[base + i]
        tok = jnp.where(a >= n_tok, a - n_tok, a)
        return pltpu.make_async_copy(x_hbm.at[pl.ds(tok, 1), :], acc.at[pl.ds(i, 1), :], gsem)

    def scatter_copy(i):
        return pltpu.make_async_copy(acc.at[pl.ds(i, 1), :], y_hbm.at[pl.ds(perm_ref[base + i], 1), :], ssem)

    def for_rows(fn):
        groups = rows // DMA_UNROLL

        def group_body(gi, carry):
            for u in range(DMA_UNROLL):
                fn(gi * DMA_UNROLL + u)
            return carry

        def row_body(i, carry):
            fn(i)
            return carry

        lax.fori_loop(0, groups, group_body, 0)
        lax.fori_loop(groups * DMA_UNROLL, rows, row_body, 0)

    def start_all(make):
        for_rows(lambda i: make(i).start())

    def wait_all(make):
        for_rows(lambda i: make(i).wait())

    @pl.when(rows > 0)
    def _():
        @pl.when(f == 0)
        def _():
            acc[...] = jnp.zeros_like(acc)
            start_all(gather_copy)
            wait_all(gather_copy)
            xb[...] = acc[...].astype(BF16)
            acc[...] = jnp.zeros_like(acc)

        lo = 0
        for m in _moe_row_buckets(tm):
            @pl.when((rows > lo) & (rows <= m))
            def _(m=m):
                acc[0:m, :] += _swiglu_step(xb[0:m, :], w1_ref, w3_ref, w2_ref)
            lo = m

        @pl.when(f == pl.num_programs(1) - 1)
        def _():
            start_all(scatter_copy)
            wait_all(scatter_copy)


def _moe_row_buckets(tm):
    return tuple(range(2 * MOE_ROW_STEP, tm, MOE_ROW_STEP)) + (tm,)


def moe_experts(x, w1, w3, w2, layer, tile_expert, tile_rows, perm, tm, tf=256):
    s, d = x.shape
    dff = w1.shape[3]
    n_tiles = tile_expert.shape[0]
    nf = dff // tf
    last = nf - 1

    def fsel(t, f, trows):
        return jnp.where(trows[t] > 0, f, last)

    grid_spec = pltpu.PrefetchScalarGridSpec(
        num_scalar_prefetch=3,
        grid=(n_tiles, nf),
        in_specs=[pl.BlockSpec(memory_space=pl.ANY),
                  pl.BlockSpec((None, None, d, tf), lambda t, f, te, tr, pm: (layer, te[t], 0, fsel(t, f, tr))),
                  pl.BlockSpec((None, None, d, tf), lambda t, f, te, tr, pm: (layer, te[t], 0, fsel(t, f, tr))),
                  pl.BlockSpec((None, None, tf, d), lambda t, f, te, tr, pm: (layer, te[t], fsel(t, f, tr), 0))],
        out_specs=pl.BlockSpec(memory_space=pl.ANY),
        scratch_shapes=[pltpu.VMEM((tm, d), BF16),
                        pltpu.VMEM((tm, d), F32),
                        pltpu.SemaphoreType.DMA(()),
                        pltpu.SemaphoreType.DMA(())],
    )
    return pl.pallas_call(
        _moe_kernel,
        grid_spec=grid_spec,
        out_shape=jax.ShapeDtypeStruct((TOP_K * s, d), F32),
        compiler_params=_cparams(("arbitrary", "arbitrary")),
    )(tile_expert, tile_rows, perm, x, w1, w3, w2)


def _route_plan(e1, e2, tm, n_tiles):
    s = e1.shape[0]
    experts = jnp.concatenate([e1, e2])
    onehot = (experts[:, None] == jnp.arange(N_EXPERTS, dtype=jnp.int32)[None, :]).astype(jnp.int32)
    csum = jnp.cumsum(onehot, axis=0)
    counts = csum[-1]
    tiles_per = (counts + tm - 1) // tm
    tile_end = jnp.cumsum(tiles_per)
    tile_start = tile_end - tiles_per
    per_tile = (counts + jnp.maximum(tiles_per, 1) - 1) // jnp.maximum(tiles_per, 1)
    rank = jnp.sum(onehot * (csum - 1), axis=1)
    own_start = jnp.sum(onehot * tile_start[None, :], axis=1)
    own_per_tile = jnp.maximum(jnp.sum(onehot * per_tile[None, :], axis=1), 1)
    pos = (own_start + rank // own_per_tile) * tm + rank % own_per_tile
    n_rows = n_tiles * tm
    perm = jnp.zeros((n_rows,), jnp.int32).at[pos].set(jnp.arange(TOP_K * s, dtype=jnp.int32))
    tid = jnp.arange(n_tiles, dtype=jnp.int32)
    texp = jnp.minimum(jnp.sum((tid[:, None] >= tile_end[None, :]).astype(jnp.int32), axis=1), N_EXPERTS - 1)
    used = tid < tile_end[-1]
    local = tid - tile_start[texp]
    trows = jnp.where(used, jnp.clip(counts[texp] - local * per_tile[texp], 0, per_tile[texp]), 0).astype(jnp.int32)
    last_used = texp[jnp.maximum(tile_end[-1] - 1, 0)]
    texp = jnp.where(used, texp, last_used).astype(jnp.int32)
    return texp, trows, perm


def _combine_kernel(h_ref, y1_ref, y2_ref, gt_ref, g_ref, *out_refs, want_h):
    gates = gt_ref[...]
    hnew = h_ref[...] + gates[:, 2:3] * y1_ref[...] + gates[:, 3:4] * y2_ref[...]
    if want_h:
        out_refs[0][...] = hnew
    out_refs[-1][...] = _rms(hnew, g_ref[...]).astype(out_refs[-1].dtype)


def combine_residual_norm(h, y, route, g, want_h, norm_dtype, tm=512):
    s, d = h.shape
    nb = s // tm
    row = lambda m: (m, 0)
    outs_spec = [pl.BlockSpec((tm, d), row)]
    outs_shape = [jax.ShapeDtypeStruct((s, d), norm_dtype)]
    if want_h:
        outs_spec = [pl.BlockSpec((tm, d), row)] + outs_spec
        outs_shape = [jax.ShapeDtypeStruct((s, d), F32)] + outs_shape
    return pl.pallas_call(
        functools.partial(_combine_kernel, want_h=want_h),
        grid=(nb,),
        in_specs=[pl.BlockSpec((tm, d), row),
                  pl.BlockSpec((tm, d), row),
                  pl.BlockSpec((tm, d), lambda m: (m + nb, 0)),
                  pl.BlockSpec((tm, route.shape[1]), row),
                  pl.BlockSpec((1, d), lambda m: (0, 0))],
        out_specs=outs_spec,
        out_shape=outs_shape,
        compiler_params=_cparams(("parallel",)),
    )(h, y, y, route, g.reshape(1, d))


MOE_TM = 2176


def kernel(x, even_norm1, even_w_in, even_conv_w, even_conv_b, even_cln_g, even_cln_b, even_w_out, even_norm2, even_ffn_w1, even_ffn_w3, even_ffn_w2, odd_norm1, odd_ssm_w_in, odd_lam_re, odd_lam_im, odd_log_dt, odd_b_re, odd_b_im, odd_c_re, odd_c_im, odd_d_skip, odd_glu_wa, odd_glu_wg, odd_norm2, odd_router, odd_moe_w1, odd_moe_w3, odd_moe_w2, final_norm):
    b, s, d = x.shape
    assert b == 1
    depth = even_norm1.shape[0] + odd_norm1.shape[0]
    assert even_norm1.shape[0] == odd_norm1.shape[0]
    h = x.reshape(s, d)
    hn = rmsnorm(h, even_norm1[0], BF16)
    n_tiles = (TOP_K * s) // MOE_TM + N_EXPERTS
    out = None
    ffn_out = None
    for layer in range(depth):
        j = layer // 2
        if layer % 2 == 0:
            proj = matmul(hn, even_w_in, j)
            att = dilated_attention(proj)
            cv = conv_module(proj, even_conv_w[j], even_conv_b[j], even_cln_g[j], even_cln_b[j])
            h, hn = outproj_residual_norm(att, cv, even_w_out, j, h, even_norm2[j])
            ffn_out = ffn(hn, even_ffn_w1, even_ffn_w3, even_ffn_w2, j)
        else:
            h, u = residual_norm_matmul(h, ffn_out, odd_norm1[j], odd_ssm_w_in, j)
            y = ssm_core(u, (odd_lam_re[j], odd_lam_im[j], odd_log_dt[j], odd_b_re[j], odd_b_im[j],
                             odd_c_re[j], odd_c_im[j], odd_d_skip[j]))
            h, hn = gluproj_residual_norm(y, odd_glu_wa, odd_glu_wg, j, h, odd_norm2[j])
            route = router_top2(hn, odd_router[j])
            e1 = route[:, 0].astype(jnp.int32)
            e2 = route[:, 1].astype(jnp.int32)
            texp, trows, perm = _route_plan(e1, e2, MOE_TM, n_tiles)
            ys = moe_experts(hn, odd_moe_w1, odd_moe_w3, odd_moe_w2, j, texp, trows, perm, MOE_TM)
            if layer + 1 < depth:
                h, hn = combine_residual_norm(h, ys, route, even_norm1[j + 1], True, BF16)
            else:
                (out,) = combine_residual_norm(h, ys, route, final_norm, False, F32)
    return out.reshape(b, s, d)
```

```python
import functools
import math

import jax
import jax.numpy as jnp
from jax import lax
from jax.experimental import pallas as pl
from jax.experimental.pallas import tpu as pltpu

F32 = jnp.float32
BF16 = jnp.bfloat16

NORM_EPS = 1e-5
N_HEADS = 8
HEAD_DIM = 128
D_ATT = N_HEADS * HEAD_DIM
DILATIONS = (1, 4, 16)
BAND = 128
ATT_CHUNK = BAND * DILATIONS[-1]
ATT_UNROLL = {1: 16, 4: 16, 16: 8}
ALIBI_MAX_BIAS = 8.0
D_CONV = 1024
CONV_WIDTH = 31
CONV_HALO = 32
SSM_GROUP = 16
SSM_STATE = 64
SSM_CHUNK = 16
SSM_LANES = 128
N_EXPERTS = 8
TOP_K = 2
DMA_UNROLL = 8
MOE_ROW_STEP = 128

VMEM_LIMIT = 56 * 1024 * 1024


def _cparams(sem):
    return pltpu.CompilerParams(dimension_semantics=sem, vmem_limit_bytes=VMEM_LIMIT)


def _rms(x, g):
    return x * lax.rsqrt(jnp.mean(x * x, axis=-1, keepdims=True) + NORM_EPS) * g


def _rmsnorm_kernel(x_ref, g_ref, o_ref):
    o_ref[...] = _rms(x_ref[...], g_ref[...]).astype(o_ref.dtype)


def rmsnorm(x, g, out_dtype, tm=512):
    s, d = x.shape
    return pl.pallas_call(
        _rmsnorm_kernel,
        grid=(s // tm,),
        in_specs=[pl.BlockSpec((tm, d), lambda m: (m, 0)),
                  pl.BlockSpec((1, d), lambda m: (0, 0))],
        out_specs=pl.BlockSpec((tm, d), lambda m: (m, 0)),
        out_shape=jax.ShapeDtypeStruct((s, d), out_dtype),
        compiler_params=_cparams(("parallel",)),
    )(x, g.reshape(1, d))


def _matmul_kernel(x_ref, w_ref, o_ref, wb_ref):
    @pl.when(pl.program_id(1) == 0)
    def _():
        wb_ref[...] = w_ref[...].astype(BF16)

    o_ref[...] = jnp.dot(x_ref[...], wb_ref[...], preferred_element_type=F32).astype(o_ref.dtype)


def matmul(x, w, layer, out_dtype=F32, tm=2048, tn=512):
    s, k = x.shape
    n = w.shape[2]
    tn = min(tn, n)
    return pl.pallas_call(
        _matmul_kernel,
        grid=(n // tn, s // tm),
        in_specs=[pl.BlockSpec((tm, k), lambda j, m: (m, 0)),
                  pl.BlockSpec((None, k, tn), lambda j, m: (layer, 0, j))],
        out_specs=pl.BlockSpec((tm, tn), lambda j, m: (m, j)),
        out_shape=jax.ShapeDtypeStruct((s, n), out_dtype),
        scratch_shapes=[pltpu.VMEM((k, tn), BF16)],
        compiler_params=_cparams(("arbitrary", "arbitrary")),
    )(x, w)


def _resnorm_matmul_kernel(h_ref, y_ref, g_ref, w_ref, hnew_ref, o_ref, wb_ref):
    @pl.when(pl.program_id(0) == 0)
    def _():
        wb_ref[...] = w_ref[...].astype(BF16)

    hnew = h_ref[...] + y_ref[...]
    hnew_ref[...] = hnew
    xn = _rms(hnew, g_ref[...]).astype(BF16)
    o_ref[...] = jnp.dot(xn, wb_ref[...], preferred_element_type=F32).astype(o_ref.dtype)


def residual_norm_matmul(h, y, g, w, layer, tm=512):
    s, d = h.shape
    n = w.shape[2]
    row = lambda m: (m, 0)
    return pl.pallas_call(
        _resnorm_matmul_kernel,
        grid=(s // tm,),
        in_specs=[pl.BlockSpec((tm, d), row),
                  pl.BlockSpec((tm, d), row),
                  pl.BlockSpec((1, d), lambda m: (0, 0)),
                  pl.BlockSpec((None, d, n), lambda m: (layer, 0, 0), pipeline_mode=pl.Buffered(1))],
        out_specs=[pl.BlockSpec((tm, d), row), pl.BlockSpec((tm, n), row)],
        out_shape=[jax.ShapeDtypeStruct((s, d), F32), jax.ShapeDtypeStruct((s, n), F32)],
        scratch_shapes=[pltpu.VMEM((d, n), BF16)],
        compiler_params=_cparams(("arbitrary",)),
    )(h, y, g.reshape(1, d), w)


def _attn_kernel(q_ref, kp_ref, kc_ref, vp_ref, vc_ref, o_ref, kk, vv, osc, lsc):
    h = pl.program_id(0)
    c = pl.program_id(1)
    kk[0:ATT_CHUNK, :] = kp_ref[...]
    kk[ATT_CHUNK:2 * ATT_CHUNK, :] = kc_ref[...]
    vv[0:ATT_CHUNK, :] = vp_ref[...]
    vv[ATT_CHUNK:2 * ATT_CHUNK, :] = vc_ref[...]

    qi = lax.broadcasted_iota(jnp.int32, (BAND, 2 * BAND), 0)
    kj = lax.broadcasted_iota(jnp.int32, (BAND, 2 * BAND), 1)
    dist = BAND + qi - kj
    band = (dist >= 0) & (dist <= BAND)
    head = jnp.full((BAND, 2 * BAND), h + 1, jnp.int32).astype(F32)
    slope = jnp.exp2(-(ALIBI_MAX_BIAS / N_HEADS) * head)
    sdist = slope * dist.astype(F32)
    scale = HEAD_DIM ** -0.5
    blocks = ATT_CHUNK // BAND

    for g, d in enumerate(DILATIONS):
        bias = jnp.where(band, -sdist * float(d), -jnp.inf)
        bias_first = jnp.where(kj >= BAND, bias, -jnp.inf)

        def one_block(b, g=g, d=d, bias=bias, bias_first=bias_first):
            n = b // d
            r = b % d
            qstart = n * (BAND * d) + r
            kstart = ATT_CHUNK + (n - 1) * (BAND * d) + r
            q = q_ref[pl.ds(qstart, BAND, stride=d), :].astype(BF16)
            k = kk[pl.ds(kstart, 2 * BAND, stride=d), :].astype(BF16)
            v = vv[pl.ds(kstart, 2 * BAND, stride=d), :].astype(BF16)
            s = lax.dot_general(q, k, (((1,), (1,)), ((), ())), preferred_element_type=F32) * scale
            s = s + jnp.where((c == 0) & (n == 0), bias_first, bias)
            m = jnp.max(s, axis=-1, keepdims=True)
            p = jnp.exp(s - m)
            den = jnp.sum(p, axis=-1, keepdims=True)
            o = jnp.dot(p.astype(BF16), v, preferred_element_type=F32) / den
            lse = m + jnp.log(den)
            osc[g, pl.ds(qstart, BAND, stride=d), :] = o
            lsc[g, pl.ds(qstart, BAND, stride=d), :] = jnp.broadcast_to(lse, (BAND, HEAD_DIM))

        unroll = ATT_UNROLL[d]

        def body(i, carry, one_block=one_block, unroll=unroll):
            for u in range(unroll):
                one_block(i * unroll + u)
            return carry

        lax.fori_loop(0, blocks // unroll, body, 0)

    l0, l1, l2 = lsc[0], lsc[1], lsc[2]
    m = jnp.maximum(jnp.maximum(l0, l1), l2)
    w0, w1, w2 = jnp.exp(l0 - m), jnp.exp(l1 - m), jnp.exp(l2 - m)
    att = (w0 * osc[0] + w1 * osc[1] + w2 * osc[2]) / (w0 + w1 + w2)
    o_ref[...] = att.astype(o_ref.dtype)


def dilated_attention(proj):
    s = proj.shape[0]
    blk = (ATT_CHUNK, HEAD_DIM)
    prev = lambda c: jnp.maximum(c - 1, 0)
    return pl.pallas_call(
        _attn_kernel,
        grid=(N_HEADS, s // ATT_CHUNK),
        in_specs=[pl.BlockSpec(blk, lambda h, c: (c, h)),
                  pl.BlockSpec(blk, lambda h, c: (prev(c), N_HEADS + h)),
                  pl.BlockSpec(blk, lambda h, c: (c, N_HEADS + h)),
                  pl.BlockSpec(blk, lambda h, c: (prev(c), 2 * N_HEADS + h)),
                  pl.BlockSpec(blk, lambda h, c: (c, 2 * N_HEADS + h))],
        out_specs=pl.BlockSpec(blk, lambda h, c: (c, h)),
        out_shape=jax.ShapeDtypeStruct((s, D_ATT), BF16),
        scratch_shapes=[pltpu.VMEM((2 * ATT_CHUNK, HEAD_DIM), F32),
                        pltpu.VMEM((2 * ATT_CHUNK, HEAD_DIM), F32),
                        pltpu.VMEM((len(DILATIONS), ATT_CHUNK, HEAD_DIM), F32),
                        pltpu.VMEM((len(DILATIONS), ATT_CHUNK, HEAD_DIM), F32)],
        compiler_params=_cparams(("parallel", "parallel")),
    )(proj, proj, proj, proj, proj)


def _conv_kernel(val_ref, gate_ref, hval_ref, hgate_ref, w_ref, b_ref, lg_ref, lb_ref, o_ref, glu):
    m = pl.program_id(0)
    tm = val_ref.shape[0]
    halo = hval_ref[...] * jax.nn.sigmoid(hgate_ref[...])
    glu[0:CONV_HALO, :] = jnp.where(m > 0, halo, 0.0)
    glu[CONV_HALO:CONV_HALO + tm, :] = val_ref[...] * jax.nn.sigmoid(gate_ref[...])
    first = CONV_HALO - (CONV_WIDTH - 1)

    acc = jnp.broadcast_to(b_ref[...], (tm, D_CONV))
    for j in range(CONV_WIDTH):
        acc = acc + w_ref[j:j + 1, :] * glu[pl.ds(first + j, tm), :]
    mu = jnp.mean(acc, axis=-1, keepdims=True)
    cen = acc - mu
    var = jnp.mean(cen * cen, axis=-1, keepdims=True)
    y = cen * lax.rsqrt(var + NORM_EPS) * lg_ref[...] + lb_ref[...]
    o_ref[...] = (y * jax.nn.sigmoid(y)).astype(o_ref.dtype)


def conv_module(proj, conv_w, conv_b, ln_g, ln_b, tm=512):
    s = proj.shape[0]
    vcol = 3 * D_ATT // D_CONV
    gcol = vcol + 1
    per = tm // CONV_HALO
    hrow = lambda m: jnp.maximum(m * per - 1, 0)
    vec = pl.BlockSpec((1, D_CONV), lambda m: (0, 0))
    return pl.pallas_call(
        _conv_kernel,
        grid=(s // tm,),
        in_specs=[pl.BlockSpec((tm, D_CONV), lambda m: (m, vcol)),
                  pl.BlockSpec((tm, D_CONV), lambda m: (m, gcol)),
                  pl.BlockSpec((CONV_HALO, D_CONV), lambda m: (hrow(m), vcol)),
                  pl.BlockSpec((CONV_HALO, D_CONV), lambda m: (hrow(m), gcol)),
                  pl.BlockSpec((CONV_WIDTH, D_CONV), lambda m: (0, 0)),
                  vec, vec, vec],
        out_specs=pl.BlockSpec((tm, D_CONV), lambda m: (m, 0)),
        out_shape=jax.ShapeDtypeStruct((s, D_CONV), BF16),
        scratch_shapes=[pltpu.VMEM((CONV_HALO + tm, D_CONV), F32)],
        compiler_params=_cparams(("parallel",)),
    )(proj, proj, proj, proj, conv_w, conv_b.reshape(1, -1), ln_g.reshape(1, -1), ln_b.reshape(1, -1))


def _outproj_kernel(att_ref, cv_ref, w_ref, h_ref, g_ref, hnew_ref, hn_ref, wb_ref):
    @pl.when(pl.program_id(0) == 0)
    def _():
        wb_ref[...] = w_ref[...].astype(BF16)

    y = jnp.dot(att_ref[...], wb_ref[0:D_ATT, :], preferred_element_type=F32)
    y = y + jnp.dot(cv_ref[...], wb_ref[D_ATT:D_ATT + D_CONV, :], preferred_element_type=F32)
    hnew = h_ref[...] + y
    hnew_ref[...] = hnew
    hn_ref[...] = _rms(hnew, g_ref[...]).astype(hn_ref.dtype)


def outproj_residual_norm(att, cv, w_out, layer, h, g, tm=512):
    s, d = h.shape
    kin = w_out.shape[1]
    row = lambda m: (m, 0)
    fixed = lambda m: (0, 0)
    return pl.pallas_call(
        _outproj_kernel,
        grid=(s // tm,),
        in_specs=[pl.BlockSpec((tm, D_ATT), row),
                  pl.BlockSpec((tm, D_CONV), row),
                  pl.BlockSpec((None, kin, d), lambda m: (layer, 0, 0), pipeline_mode=pl.Buffered(1)),
                  pl.BlockSpec((tm, d), row),
                  pl.BlockSpec((1, d), fixed)],
        out_specs=[pl.BlockSpec((tm, d), row), pl.BlockSpec((tm, d), row)],
        out_shape=[jax.ShapeDtypeStruct((s, d), F32), jax.ShapeDtypeStruct((s, d), BF16)],
        scratch_shapes=[pltpu.VMEM((kin, d), BF16)],
        compiler_params=_cparams(("arbitrary",)),
    )(att, cv, w_out, h, g.reshape(1, d))


def _gluproj_kernel(y_ref, wa_ref, wg_ref, h_ref, g_ref, hnew_ref, hn_ref, wab_ref, wgb_ref):
    @pl.when(pl.program_id(0) == 0)
    def _():
        wab_ref[...] = wa_ref[...].astype(BF16)
        wgb_ref[...] = wg_ref[...].astype(BF16)

    y = y_ref[...].astype(BF16)
    a = jnp.dot(y, wab_ref[...], preferred_element_type=F32)
    gt = jnp.dot(y, wgb_ref[...], preferred_element_type=F32)
    hnew = h_ref[...] + a * jax.nn.sigmoid(gt)
    hnew_ref[...] = hnew
    hn_ref[...] = _rms(hnew, g_ref[...]).astype(hn_ref.dtype)


def gluproj_residual_norm(y, wa, wg, layer, h, g, tm=256):
    s, d = h.shape
    kin = wa.shape[1]
    row = lambda m: (m, 0)
    fixed = lambda m: (0, 0)
    once = lambda m: (layer, 0, 0)
    return pl.pallas_call(
        _gluproj_kernel,
        grid=(s // tm,),
        in_specs=[pl.BlockSpec((tm, kin), row),
                  pl.BlockSpec((None, kin, d), once, pipeline_mode=pl.Buffered(1)),
                  pl.BlockSpec((None, kin, d), once, pipeline_mode=pl.Buffered(1)),
                  pl.BlockSpec((tm, d), row),
                  pl.BlockSpec((1, d), fixed)],
        out_specs=[pl.BlockSpec((tm, d), row), pl.BlockSpec((tm, d), row)],
        out_shape=[jax.ShapeDtypeStruct((s, d), F32), jax.ShapeDtypeStruct((s, d), F32)],
        scratch_shapes=[pltpu.VMEM((kin, d), BF16), pltpu.VMEM((kin, d), BF16)],
        compiler_params=_cparams(("arbitrary",)),
    )(y, wa, wg, h, g.reshape(1, d))


def _swiglu_step(xb, w1_ref, w3_ref, w2_ref):
    a = jnp.dot(xb, w1_ref[...].astype(BF16), preferred_element_type=F32)
    b = jnp.dot(xb, w3_ref[...].astype(BF16), preferred_element_type=F32)
    p = (a * jax.nn.sigmoid(a) * b).astype(BF16)
    return jnp.dot(p, w2_ref[...].astype(BF16), preferred_element_type=F32)


def _ffn_kernel(x_ref, w1_ref, w3_ref, w2_ref, y_ref):
    @pl.when(pl.program_id(1) == 0)
    def _():
        y_ref[...] = jnp.zeros_like(y_ref)

    y_ref[...] += _swiglu_step(x_ref[...], w1_ref, w3_ref, w2_ref)


def ffn(x, w1, w3, w2, layer, tm=1024, tf=256):
    s, d = x.shape
    dff = w1.shape[2]
    row = lambda m, f: (m, 0)
    return pl.pallas_call(
        _ffn_kernel,
        grid=(s // tm, dff // tf),
        in_specs=[pl.BlockSpec((tm, d), row),
                  pl.BlockSpec((None, d, tf), lambda m, f: (layer, 0, f)),
                  pl.BlockSpec((None, d, tf), lambda m, f: (layer, 0, f)),
                  pl.BlockSpec((None, tf, d), lambda m, f: (layer, f, 0))],
        out_specs=pl.BlockSpec((tm, d), row),
        out_shape=jax.ShapeDtypeStruct((s, d), F32),
        compiler_params=_cparams(("parallel", "arbitrary")),
    )(x, w1, w3, w2)


def _ssm_tables(lam_re, lam_im, log_dt, b_re, b_im, c_re, c_im, d_skip, n_doublings):
    L = SSM_CHUNK
    G, P = lam_re.shape
    dt = jnp.exp(log_dt)[:, None]
    mag = jnp.exp(lam_re * dt)
    a_re = mag * jnp.cos(lam_im * dt)
    a_im = mag * jnp.sin(lam_im * dt)
    inv = 1.0 / (lam_re * lam_re + lam_im * lam_im)
    f_re = ((a_re - 1.0) * lam_re + a_im * lam_im) * inv
    f_im = (a_im * lam_re - (a_re - 1.0) * lam_im) * inv
    bb_re = f_re[..., None] * b_re - f_im[..., None] * b_im
    bb_im = f_re[..., None] * b_im + f_im[..., None] * b_re

    pw_re = [jnp.ones_like(a_re)]
    pw_im = [jnp.zeros_like(a_im)]
    for _ in range(L):
        pr, pi = pw_re[-1], pw_im[-1]
        pw_re.append(pr * a_re - pi * a_im)
        pw_im.append(pr * a_im + pi * a_re)
    pw_re = jnp.stack(pw_re)
    pw_im = jnp.stack(pw_im)

    ab_re = pw_re[:L, :, :, None] * bb_re[None] - pw_im[:L, :, :, None] * bb_im[None]
    ab_im = pw_re[:L, :, :, None] * bb_im[None] + pw_im[:L, :, :, None] * bb_re[None]
    hi = lax.Precision.HIGHEST
    kern = (jnp.einsum('gcp,tgpd->tgcd', c_re, ab_re, precision=hi)
            - jnp.einsum('gcp,tgpd->tgcd', c_im, ab_im, precision=hi))
    C = SSM_GROUP
    Q = SSM_LANES // C
    O = G // Q
    kern = kern.at[0].add(jnp.eye(C, dtype=F32)[None] * d_skip.reshape(G, C)[:, :, None])

    kc = kern.reshape(L, O, Q, C, C).transpose(1, 4, 0, 2, 3).reshape(O, C, L * Q * C)

    def by_state(t):
        return jnp.stack(t, axis=3).reshape(O, L, C, 2 * Q * P)

    bc = by_state([ab[::-1].reshape(L, O, Q, P, C).transpose(1, 0, 4, 2, 3) for ab in (ab_re, ab_im)])

    ca_re = c_re[None] * pw_re[1:, :, None, :] - c_im[None] * pw_im[1:, :, None, :]
    ca_im = c_re[None] * pw_im[1:, :, None, :] + c_im[None] * pw_re[1:, :, None, :]
    cc = by_state([ca.reshape(L, O, Q, C, P).transpose(1, 0, 3, 2, 4) for ca in (ca_re, -ca_im)])

    sq_re, sq_im = pw_re[L].reshape(O, Q * P), pw_im[L].reshape(O, Q * P)
    same, cross = [], []
    for _ in range(n_doublings):
        same.append(jnp.concatenate([sq_re, sq_re], axis=-1))
        cross.append(jnp.concatenate([-sq_im, sq_im], axis=-1))
        sq_re, sq_im = sq_re * sq_re - sq_im * sq_im, 2.0 * sq_re * sq_im
    return kc, bc, cc, jnp.stack(same, axis=1), jnp.stack(cross, axis=1)


def _ssm_kernel(u_ref, kc_ref, bc_ref, cc_ref, same_ref, cross_ref, y_ref, yacc, kcat):
    L = SSM_CHUNK
    nc = yacc.shape[0]
    n_state = same_ref.shape[1]
    half = n_state // 2
    groups = SSM_LANES // SSM_GROUP

    def spread(table, col_group):
        rows = lax.broadcasted_iota(jnp.int32, (SSM_LANES, table.shape[1]), 0)
        cols = lax.broadcasted_iota(jnp.int32, (SSM_LANES, table.shape[1]), 1)
        tiled = jnp.concatenate([table] * groups, axis=0)
        return jnp.where(rows // SSM_GROUP == col_group(cols), tiled, 0.0).astype(BF16)

    state_group = lambda cols: (cols % half) // SSM_STATE
    lane_group = lambda cols: (cols % SSM_LANES) // SSM_GROUP

    kcat[...] = spread(kc_ref[...], lane_group)
    yacc[...] = jnp.zeros_like(yacc)
    x = jnp.zeros((nc, n_state), F32)
    for j in range(L):
        uj = u_ref[pl.ds(j, nc, stride=L), :].astype(BF16)
        x = x + jnp.dot(uj, spread(bc_ref[j], state_group), preferred_element_type=F32)
        width = (L - j) * SSM_LANES
        yacc[:, j * SSM_LANES:] += jnp.dot(uj, kcat[:, :width], preferred_element_type=F32)
    row = lax.broadcasted_iota(jnp.int32, x.shape, 0)
    for k in range(same_ref.shape[0]):
        shift = 1 << k
        sh = jnp.where(row >= shift, pltpu.roll(x, shift, 0), 0.0)
        x = x + same_ref[k:k + 1, :] * sh + cross_ref[k:k + 1, :] * pltpu.roll(sh, half, 1)
    prev = jnp.where(row >= 1, pltpu.roll(x, 1, 0), 0.0).astype(BF16)
    for s in range(L):
        carried = lax.dot_general(prev, spread(cc_ref[s], state_group), (((1,), (1,)), ((), ())),
                                  preferred_element_type=F32)
        y_ref[pl.ds(s, nc, stride=L), :] = jax.nn.gelu(yacc[:, s * SSM_LANES:(s + 1) * SSM_LANES] + carried)


def ssm_core(u, params):
    s, width = u.shape
    L = SSM_CHUNK
    nc = s // L
    n_doublings = max(1, (nc - 1).bit_length())
    kc, bc, cc, same, cross = _ssm_tables(*params, n_doublings=n_doublings)
    n_blocks = width // SSM_LANES
    n_state = same.shape[2]
    blk3 = lambda o: (o, 0, 0)
    blk4 = lambda o: (o, 0, 0, 0)
    return pl.pallas_call(
        _ssm_kernel,
        grid=(n_blocks,),
        in_specs=[pl.BlockSpec((s, SSM_LANES), lambda o: (0, o)),
                  pl.BlockSpec((None, SSM_GROUP, L * SSM_LANES), blk3),
                  pl.BlockSpec((None, L, SSM_GROUP, n_state), blk4),
                  pl.BlockSpec((None, L, SSM_GROUP, n_state), blk4),
                  pl.BlockSpec((None, n_doublings, n_state), blk3),
                  pl.BlockSpec((None, n_doublings, n_state), blk3)],
        out_specs=pl.BlockSpec((s, SSM_LANES), lambda o: (0, o)),
        out_shape=jax.ShapeDtypeStruct((s, width), F32),
        scratch_shapes=[pltpu.VMEM((nc, L * SSM_LANES), F32),
                        pltpu.VMEM((SSM_LANES, L * SSM_LANES), BF16)],
        compiler_params=_cparams(("parallel",)),
    )(u, kc, bc, cc, same, cross)


def _router_kernel(x_ref, r_ref, o_ref):
    logits = jnp.dot(x_ref[...], r_ref[...], preferred_element_type=F32, precision=lax.Precision.HIGHEST)
    lane = lax.broadcasted_iota(jnp.int32, logits.shape, 1)
    logits = jnp.where(lane < N_EXPERTS, logits, -jnp.inf)
    big = jnp.int32(logits.shape[1])
    v1 = jnp.max(logits, axis=-1, keepdims=True)
    i1 = jnp.min(jnp.where(logits == v1, lane, big), axis=-1, keepdims=True)
    rest = jnp.where(lane == i1, -jnp.inf, logits)
    v2 = jnp.max(rest, axis=-1, keepdims=True)
    i2 = jnp.min(jnp.where(rest == v2, lane, big), axis=-1, keepdims=True)
    e2 = jnp.exp(v2 - v1)
    g1 = 1.0 / (1.0 + e2)
    g2 = e2 / (1.0 + e2)
    out = jnp.where(lane == 0, i1.astype(F32), 0.0)
    out = jnp.where(lane == 1, i2.astype(F32), out)
    out = jnp.where(lane == 2, g1, out)
    out = jnp.where(lane == 3, g2, out)
    o_ref[...] = out


def router_top2(x, router, tm=512):
    s, d = x.shape
    lanes = 128
    rp = jnp.zeros((d, lanes), F32).at[:, :N_EXPERTS].set(router)
    return pl.pallas_call(
        _router_kernel,
        grid=(s // tm,),
        in_specs=[pl.BlockSpec((tm, d), lambda m: (m, 0)),
                  pl.BlockSpec((d, lanes), lambda m: (0, 0))],
        out_specs=pl.BlockSpec((tm, lanes), lambda m: (m, 0)),
        out_shape=jax.ShapeDtypeStruct((s, lanes), F32),
        compiler_params=_cparams(("parallel",)),
    )(x, rp)


def _moe_kernel(texp_ref, trows_ref, perm_ref,
                x_hbm, w1_ref, w3_ref, w2_ref, y_hbm, xb, land, acc, gsem, ssem):
    t = pl.program_id(0)
    f = pl.program_id(1)
    n_t = pl.num_programs(0)
    last_f = pl.num_programs(1) - 1
    tm = land.shape[0]
    n_tok = x_hbm.shape[0]
    rows = trows_ref[t]
    acc_t = acc.at[t % 2]

    def gather_copy(tile, i):
        a = perm_ref[tile * tm + i]
        tok = jnp.where(a >= n_tok, a - n_tok, a)
        return pltpu.make_async_copy(x_hbm.at[pl.ds(tok, 1), :], land.at[pl.ds(i, 1), :], gsem)

    def scatter_copy(tile, i):
        return pltpu.make_async_copy(acc.at[tile % 2, pl.ds(i, 1), :],
                                     y_hbm.at[pl.ds(perm_ref[tile * tm + i], 1), :], ssem.at[tile % 2])

    def for_rows(n, fn):
        groups = n // DMA_UNROLL

        def group_body(gi, carry):
            for u in range(DMA_UNROLL):
                fn(gi * DMA_UNROLL + u)
            return carry

        def row_body(i, carry):
            fn(i)
            return carry

        lax.fori_loop(0, groups, group_body, 0)
        lax.fori_loop(groups * DMA_UNROLL, n, row_body, 0)

    @pl.when(f == 0)
    def _():
        @pl.when(t == 0)
        def _():
            land[...] = jnp.zeros_like(land)
            for_rows(rows, lambda i: gather_copy(t, i).start())

        for_rows(rows, lambda i: gather_copy(t, i).wait())
        xb[...] = land[...].astype(BF16)

        @pl.when(t + 1 < n_t)
        def _():
            for_rows(trows_ref[t + 1], lambda i: gather_copy(t + 1, i).start())

        acc_t[...] = jnp.zeros_like(acc_t)

    lo = 0
    for m in _moe_row_buckets(tm):
        @pl.when((rows > lo) & (rows <= m))
        def _(m=m):
            acc_t[0:m, :] += _swiglu_step(xb[0:m, :], w1_ref, w3_ref, w2_ref)
        lo = m

    @pl.when(f == last_f)
    def _():
        @pl.when(t >= 1)
        def _():
            for_rows(trows_ref[t - 1], lambda i: scatter_copy(t - 1, i).wait())

        for_rows(rows, lambda i: scatter_copy(t, i).start())

        @pl.when(t == n_t - 1)
        def _():
            for_rows(rows, lambda i: scatter_copy(t, i).wait())


def _moe_row_buckets(tm):
    return tuple(range(2 * MOE_ROW_STEP, tm, MOE_ROW_STEP)) + (tm,)


def moe_experts(x, w1, w3, w2, layer, tile_expert, tile_rows, perm, tm, tf=256):
    s, d = x.shape
    dff = w1.shape[3]
    n_tiles = tile_expert.shape[0]
    nf = dff // tf
    last = nf - 1

    def fsel(t, f, trows):
        return jnp.where(trows[t] > 0, f, last)

    grid_spec = pltpu.PrefetchScalarGridSpec(
        num_scalar_prefetch=3,
        grid=(n_tiles, nf),
        in_specs=[pl.BlockSpec(memory_space=pl.ANY),
                  pl.BlockSpec((None, None, d, tf), lambda t, f, te, tr, pm: (layer, te[t], 0, fsel(t, f, tr))),
                  pl.BlockSpec((None, None, d, tf), lambda t, f, te, tr, pm: (layer, te[t], 0, fsel(t, f, tr))),
                  pl.BlockSpec((None, None, tf, d), lambda t, f, te, tr, pm: (layer, te[t], fsel(t, f, tr), 0))],
        out_specs=pl.BlockSpec(memory_space=pl.ANY),
        scratch_shapes=[pltpu.VMEM((tm, d), BF16),
                        pltpu.VMEM((tm, d), F32),
                        pltpu.VMEM((2, tm, d), F32),
                        pltpu.SemaphoreType.DMA(()),
                        pltpu.SemaphoreType.DMA((2,))],
    )
    return pl.pallas_call(
        _moe_kernel,
        grid_spec=grid_spec,
        out_shape=jax.ShapeDtypeStruct((TOP_K * s, d), F32),
        compiler_params=_cparams(("arbitrary", "arbitrary")),
    )(tile_expert, tile_rows, perm, x, w1, w3, w2)


def _route_plan(e1, e2, tm, n_tiles):
    s = e1.shape[0]
    experts = jnp.concatenate([e1, e2])
    onehot = (experts[:, None] == jnp.arange(N_EXPERTS, dtype=jnp.int32)[None, :]).astype(jnp.int32)
    csum = jnp.cumsum(onehot, axis=0)
    counts = csum[-1]
    tiles_per = (counts + tm - 1) // tm
    tile_end = jnp.cumsum(tiles_per)
    tile_start = tile_end - tiles_per
    per_tile = (counts + jnp.maximum(tiles_per, 1) - 1) // jnp.maximum(tiles_per, 1)
    rank = jnp.sum(onehot * (csum - 1), axis=1)
    own_start = jnp.sum(onehot * tile_start[None, :], axis=1)
    own_per_tile = jnp.maximum(jnp.sum(onehot * per_tile[None, :], axis=1), 1)
    pos = (own_start + rank // own_per_tile) * tm + rank % own_per_tile
    n_rows = n_tiles * tm
    perm = jnp.zeros((n_rows,), jnp.int32).at[pos].set(jnp.arange(TOP_K * s, dtype=jnp.int32))
    tid = jnp.arange(n_tiles, dtype=jnp.int32)
    texp = jnp.minimum(jnp.sum((tid[:, None] >= tile_end[None, :]).astype(jnp.int32), axis=1), N_EXPERTS - 1)
    used = tid < tile_end[-1]
    local = tid - tile_start[texp]
    trows = jnp.where(used, jnp.clip(counts[texp] - local * per_tile[texp], 0, per_tile[texp]), 0).astype(jnp.int32)
    last_used = texp[jnp.maximum(tile_end[-1] - 1, 0)]
    texp = jnp.where(used, texp, last_used).astype(jnp.int32)
    return texp, trows, perm


def _combine_kernel(h_ref, y1_ref, y2_ref, gt_ref, g_ref, *out_refs, want_h):
    gates = gt_ref[...]
    hnew = h_ref[...] + gates[:, 2:3] * y1_ref[...] + gates[:, 3:4] * y2_ref[...]
    if want_h:
        out_refs[0][...] = hnew
    out_refs[-1][...] = _rms(hnew, g_ref[...]).astype(out_refs[-1].dtype)


def combine_residual_norm(h, y, route, g, want_h, norm_dtype, tm=512):
    s, d = h.shape
    nb = s // tm
    row = lambda m: (m, 0)
    outs_spec = [pl.BlockSpec((tm, d), row)]
    outs_shape = [jax.ShapeDtypeStruct((s, d), norm_dtype)]
    if want_h:
        outs_spec = [pl.BlockSpec((tm, d), row)] + outs_spec
        outs_shape = [jax.ShapeDtypeStruct((s, d), F32)] + outs_shape
    return pl.pallas_call(
        functools.partial(_combine_kernel, want_h=want_h),
        grid=(nb,),
        in_specs=[pl.BlockSpec((tm, d), row),
                  pl.BlockSpec((tm, d), row),
                  pl.BlockSpec((tm, d), lambda m: (m + nb, 0)),
                  pl.BlockSpec((tm, route.shape[1]), row),
                  pl.BlockSpec((1, d), lambda m: (0, 0))],
        out_specs=outs_spec,
        out_shape=outs_shape,
        compiler_params=_cparams(("parallel",)),
    )(h, y, y, route, g.reshape(1, d))


MOE_TM = 1088


def kernel(x, even_norm1, even_w_in, even_conv_w, even_conv_b, even_cln_g, even_cln_b, even_w_out, even_norm2, even_ffn_w1, even_ffn_w3, even_ffn_w2, odd_norm1, odd_ssm_w_in, odd_lam_re, odd_lam_im, odd_log_dt, odd_b_re, odd_b_im, odd_c_re, odd_c_im, odd_d_skip, odd_glu_wa, odd_glu_wg, odd_norm2, odd_router, odd_moe_w1, odd_moe_w3, odd_moe_w2, final_norm):
    b, s, d = x.shape
    assert b == 1
    depth = even_norm1.shape[0] + odd_norm1.shape[0]
    assert even_norm1.shape[0] == odd_norm1.shape[0]
    h = x.reshape(s, d)
    hn = rmsnorm(h, even_norm1[0], BF16)
    n_tiles = (TOP_K * s) // MOE_TM + N_EXPERTS
    out = None
    ffn_out = None
    for layer in range(depth):
        j = layer // 2
        if layer % 2 == 0:
            proj = matmul(hn, even_w_in, j)
            att = dilated_attention(proj)
            cv = conv_module(proj, even_conv_w[j], even_conv_b[j], even_cln_g[j], even_cln_b[j])
            h, hn = outproj_residual_norm(att, cv, even_w_out, j, h, even_norm2[j])
            ffn_out = ffn(hn, even_ffn_w1, even_ffn_w3, even_ffn_w2, j)
        else:
            h, u = residual_norm_matmul(h, ffn_out, odd_norm1[j], odd_ssm_w_in, j)
            y = ssm_core(u, (odd_lam_re[j], odd_lam_im[j], odd_log_dt[j], odd_b_re[j], odd_b_im[j],
                             odd_c_re[j], odd_c_im[j], odd_d_skip[j]))
            h, hn = gluproj_residual_norm(y, odd_glu_wa, odd_glu_wg, j, h, odd_norm2[j])
            route = router_top2(hn, odd_router[j])
            e1 = route[:, 0].astype(jnp.int32)
            e2 = route[:, 1].astype(jnp.int32)
            texp, trows, perm = _route_plan(e1, e2, MOE_TM, n_tiles)
            ys = moe_experts(hn, odd_moe_w1, odd_moe_w3, odd_moe_w2, j, texp, trows, perm, MOE_TM)
            if layer + 1 < depth:
                h, hn = combine_residual_norm(h, ys, route, even_norm1[j + 1], True, BF16)
            else:
                (out,) = combine_residual_norm(h, ys, route, final_norm, False, F32)
    return out.reshape(b, s, d)
```

```python
import functools
import math

import jax
import jax.numpy as jnp
from jax import lax
from jax.experimental import pallas as pl
from jax.experimental.pallas import tpu as pltpu

F32 = jnp.float32
BF16 = jnp.bfloat16

NORM_EPS = 1e-5
N_HEADS = 8
HEAD_DIM = 128
D_ATT = N_HEADS * HEAD_DIM
DILATIONS = (1, 4, 16)
BAND = 128
ATT_CHUNK = BAND * DILATIONS[-1]
ATT_UNROLL = {1: 16, 4: 16, 16: 8}
ALIBI_MAX_BIAS = 8.0
D_CONV = 1024
CONV_WIDTH = 31
CONV_HALO = 32
SSM_GROUP = 16
SSM_STATE = 64
SSM_CHUNK = 16
SSM_LANES = 128
N_EXPERTS = 8
TOP_K = 2
DMA_UNROLL = 8
assert DMA_UNROLL & (DMA_UNROLL - 1) == 0
MOE_ROW_STEP = 256

VMEM_LIMIT = 56 * 1024 * 1024


def _cparams(sem):
    return pltpu.CompilerParams(dimension_semantics=sem, vmem_limit_bytes=VMEM_LIMIT)


def _rms(x, g):
    return x * lax.rsqrt(jnp.mean(x * x, axis=-1, keepdims=True) + NORM_EPS) * g


def _rmsnorm_kernel(x_ref, g_ref, o_ref):
    o_ref[...] = _rms(x_ref[...], g_ref[...]).astype(o_ref.dtype)


def rmsnorm(x, g, out_dtype, tm=512):
    s, d = x.shape
    return pl.pallas_call(
        _rmsnorm_kernel,
        grid=(s // tm,),
        in_specs=[pl.BlockSpec((tm, d), lambda m: (m, 0)),
                  pl.BlockSpec((1, d), lambda m: (0, 0))],
        out_specs=pl.BlockSpec((tm, d), lambda m: (m, 0)),
        out_shape=jax.ShapeDtypeStruct((s, d), out_dtype),
        compiler_params=_cparams(("parallel",)),
    )(x, g.reshape(1, d))


def _matmul_kernel(x_ref, w_ref, o_ref, wb_ref):
    @pl.when(pl.program_id(1) == 0)
    def _():
        wb_ref[...] = w_ref[...].astype(BF16)

    o_ref[...] = jnp.dot(x_ref[...], wb_ref[...], preferred_element_type=F32).astype(o_ref.dtype)


def matmul(x, w, layer, out_dtype=F32, tm=2048, tn=512):
    s, k = x.shape
    n = w.shape[2]
    tn = min(tn, n)
    return pl.pallas_call(
        _matmul_kernel,
        grid=(n // tn, s // tm),
        in_specs=[pl.BlockSpec((tm, k), lambda j, m: (m, 0)),
                  pl.BlockSpec((None, k, tn), lambda j, m: (layer, 0, j))],
        out_specs=pl.BlockSpec((tm, tn), lambda j, m: (m, j)),
        out_shape=jax.ShapeDtypeStruct((s, n), out_dtype),
        scratch_shapes=[pltpu.VMEM((k, tn), BF16)],
        compiler_params=_cparams(("arbitrary", "arbitrary")),
    )(x, w)


def _resnorm_matmul_kernel(h_ref, y_ref, g_ref, w_ref, hnew_ref, o_ref, wb_ref):
    @pl.when(pl.program_id(0) == 0)
    def _():
        wb_ref[...] = w_ref[...].astype(BF16)

    hnew = h_ref[...] + y_ref[...]
    hnew_ref[...] = hnew
    xn = _rms(hnew, g_ref[...]).astype(BF16)
    o_ref[...] = jnp.dot(xn, wb_ref[...], preferred_element_type=F32).astype(o_ref.dtype)


def residual_norm_matmul(h, y, g, w, layer, tm=512):
    s, d = h.shape
    n = w.shape[2]
    row = lambda m: (m, 0)
    return pl.pallas_call(
        _resnorm_matmul_kernel,
        grid=(s // tm,),
        in_specs=[pl.BlockSpec((tm, d), row),
                  pl.BlockSpec((tm, d), row),
                  pl.BlockSpec((1, d), lambda m: (0, 0)),
                  pl.BlockSpec((None, d, n), lambda m: (layer, 0, 0), pipeline_mode=pl.Buffered(1))],
        out_specs=[pl.BlockSpec((tm, d), row), pl.BlockSpec((tm, n), row)],
        out_shape=[jax.ShapeDtypeStruct((s, d), F32), jax.ShapeDtypeStruct((s, n), F32)],
        scratch_shapes=[pltpu.VMEM((d, n), BF16)],
        compiler_params=_cparams(("arbitrary",)),
    )(h, y, g.reshape(1, d), w)


def _attn_kernel(q_ref, kp_ref, kc_ref, vp_ref, vc_ref, o_ref, kk, vv, osc, lsc):
    h = pl.program_id(0)
    c = pl.program_id(1)
    kk[0:ATT_CHUNK, :] = kp_ref[...]
    kk[ATT_CHUNK:2 * ATT_CHUNK, :] = kc_ref[...]
    vv[0:ATT_CHUNK, :] = vp_ref[...]
    vv[ATT_CHUNK:2 * ATT_CHUNK, :] = vc_ref[...]

    qi = lax.broadcasted_iota(jnp.int32, (BAND, 2 * BAND), 0)
    kj = lax.broadcasted_iota(jnp.int32, (BAND, 2 * BAND), 1)
    dist = BAND + qi - kj
    band = (dist >= 0) & (dist <= BAND)
    head = jnp.full((BAND, 2 * BAND), h + 1, jnp.int32).astype(F32)
    slope = jnp.exp2(-(ALIBI_MAX_BIAS / N_HEADS) * head)
    sdist = slope * dist.astype(F32)
    scale = HEAD_DIM ** -0.5
    blocks = ATT_CHUNK // BAND

    for g, d in enumerate(DILATIONS):
        bias = jnp.where(band, -sdist * float(d), -jnp.inf)
        bias_first = jnp.where(kj >= BAND, bias, -jnp.inf)

        def one_block(b, g=g, d=d, bias=bias, bias_first=bias_first):
            n = b // d
            r = b % d
            qstart = n * (BAND * d) + r
            kstart = ATT_CHUNK + (n - 1) * (BAND * d) + r
            q = q_ref[pl.ds(qstart, BAND, stride=d), :].astype(BF16)
            k = kk[pl.ds(kstart, 2 * BAND, stride=d), :].astype(BF16)
            v = vv[pl.ds(kstart, 2 * BAND, stride=d), :].astype(BF16)
            s = lax.dot_general(q, k, (((1,), (1,)), ((), ())), preferred_element_type=F32) * scale
            s = s + jnp.where((c == 0) & (n == 0), bias_first, bias)
            m = jnp.max(s, axis=-1, keepdims=True)
            p = jnp.exp(s - m)
            den = jnp.sum(p, axis=-1, keepdims=True)
            o = jnp.dot(p.astype(BF16), v, preferred_element_type=F32) / den
            lse = m + jnp.log(den)
            osc[g, pl.ds(qstart, BAND, stride=d), :] = o
            lsc[g, pl.ds(qstart, BAND, stride=d), :] = jnp.broadcast_to(lse, (BAND, HEAD_DIM))

        unroll = ATT_UNROLL[d]

        def body(i, carry, one_block=one_block, unroll=unroll):
            for u in range(unroll):
                one_block(i * unroll + u)
            return carry

        lax.fori_loop(0, blocks // unroll, body, 0)

    l0, l1, l2 = lsc[0], lsc[1], lsc[2]
    m = jnp.maximum(jnp.maximum(l0, l1), l2)
    w0, w1, w2 = jnp.exp(l0 - m), jnp.exp(l1 - m), jnp.exp(l2 - m)
    att = (w0 * osc[0] + w1 * osc[1] + w2 * osc[2]) / (w0 + w1 + w2)
    o_ref[...] = att.astype(o_ref.dtype)


def dilated_attention(proj):
    s = proj.shape[0]
    blk = (ATT_CHUNK, HEAD_DIM)
    prev = lambda c: jnp.maximum(c - 1, 0)
    return pl.pallas_call(
        _attn_kernel,
        grid=(N_HEADS, s // ATT_CHUNK),
        in_specs=[pl.BlockSpec(blk, lambda h, c: (c, h)),
                  pl.BlockSpec(blk, lambda h, c: (prev(c), N_HEADS + h)),
                  pl.BlockSpec(blk, lambda h, c: (c, N_HEADS + h)),
                  pl.BlockSpec(blk, lambda h, c: (prev(c), 2 * N_HEADS + h)),
                  pl.BlockSpec(blk, lambda h, c: (c, 2 * N_HEADS + h))],
        out_specs=pl.BlockSpec(blk, lambda h, c: (c, h)),
        out_shape=jax.ShapeDtypeStruct((s, D_ATT), BF16),
        scratch_shapes=[pltpu.VMEM((2 * ATT_CHUNK, HEAD_DIM), F32),
                        pltpu.VMEM((2 * ATT_CHUNK, HEAD_DIM), F32),
                        pltpu.VMEM((len(DILATIONS), ATT_CHUNK, HEAD_DIM), F32),
                        pltpu.VMEM((len(DILATIONS), ATT_CHUNK, HEAD_DIM), F32)],
        compiler_params=_cparams(("parallel", "parallel")),
    )(proj, proj, proj, proj, proj)


def _conv_kernel(val_ref, gate_ref, hval_ref, hgate_ref, w_ref, b_ref, lg_ref, lb_ref, o_ref, glu):
    m = pl.program_id(0)
    tm = val_ref.shape[0]
    halo = hval_ref[...] * jax.nn.sigmoid(hgate_ref[...])
    glu[0:CONV_HALO, :] = jnp.where(m > 0, halo, 0.0)
    glu[CONV_HALO:CONV_HALO + tm, :] = val_ref[...] * jax.nn.sigmoid(gate_ref[...])
    first = CONV_HALO - (CONV_WIDTH - 1)

    acc = jnp.broadcast_to(b_ref[...], (tm, D_CONV))
    for j in range(CONV_WIDTH):
        acc = acc + w_ref[j:j + 1, :] * glu[pl.ds(first + j, tm), :]
    mu = jnp.mean(acc, axis=-1, keepdims=True)
    cen = acc - mu
    var = jnp.mean(cen * cen, axis=-1, keepdims=True)
    y = cen * lax.rsqrt(var + NORM_EPS) * lg_ref[...] + lb_ref[...]
    o_ref[...] = (y * jax.nn.sigmoid(y)).astype(o_ref.dtype)


def conv_module(proj, conv_w, conv_b, ln_g, ln_b, tm=512):
    s = proj.shape[0]
    vcol = 3 * D_ATT // D_CONV
    gcol = vcol + 1
    per = tm // CONV_HALO
    hrow = lambda m: jnp.maximum(m * per - 1, 0)
    vec = pl.BlockSpec((1, D_CONV), lambda m: (0, 0))
    return pl.pallas_call(
        _conv_kernel,
        grid=(s // tm,),
        in_specs=[pl.BlockSpec((tm, D_CONV), lambda m: (m, vcol)),
                  pl.BlockSpec((tm, D_CONV), lambda m: (m, gcol)),
                  pl.BlockSpec((CONV_HALO, D_CONV), lambda m: (hrow(m), vcol)),
                  pl.BlockSpec((CONV_HALO, D_CONV), lambda m: (hrow(m), gcol)),
                  pl.BlockSpec((CONV_WIDTH, D_CONV), lambda m: (0, 0)),
                  vec, vec, vec],
        out_specs=pl.BlockSpec((tm, D_CONV), lambda m: (m, 0)),
        out_shape=jax.ShapeDtypeStruct((s, D_CONV), BF16),
        scratch_shapes=[pltpu.VMEM((CONV_HALO + tm, D_CONV), F32)],
        compiler_params=_cparams(("parallel",)),
    )(proj, proj, proj, proj, conv_w, conv_b.reshape(1, -1), ln_g.reshape(1, -1), ln_b.reshape(1, -1))


def _outproj_kernel(att_ref, cv_ref, w_ref, h_ref, g_ref, hnew_ref, hn_ref, wb_ref):
    @pl.when(pl.program_id(0) == 0)
    def _():
        wb_ref[...] = w_ref[...].astype(BF16)

    y = jnp.dot(att_ref[...], wb_ref[0:D_ATT, :], preferred_element_type=F32)
    y = y + jnp.dot(cv_ref[...], wb_ref[D_ATT:D_ATT + D_CONV, :], preferred_element_type=F32)
    hnew = h_ref[...] + y
    hnew_ref[...] = hnew
    hn_ref[...] = _rms(hnew, g_ref[...]).astype(hn_ref.dtype)


def outproj_residual_norm(att, cv, w_out, layer, h, g, tm=512):
    s, d = h.shape
    kin = w_out.shape[1]
    row = lambda m: (m, 0)
    fixed = lambda m: (0, 0)
    return pl.pallas_call(
        _outproj_kernel,
        grid=(s // tm,),
        in_specs=[pl.BlockSpec((tm, D_ATT), row),
                  pl.BlockSpec((tm, D_CONV), row),
                  pl.BlockSpec((None, kin, d), lambda m: (layer, 0, 0), pipeline_mode=pl.Buffered(1)),
                  pl.BlockSpec((tm, d), row),
                  pl.BlockSpec((1, d), fixed)],
        out_specs=[pl.BlockSpec((tm, d), row), pl.BlockSpec((tm, d), row)],
        out_shape=[jax.ShapeDtypeStruct((s, d), F32), jax.ShapeDtypeStruct((s, d), BF16)],
        scratch_shapes=[pltpu.VMEM((kin, d), BF16)],
        compiler_params=_cparams(("arbitrary",)),
    )(att, cv, w_out, h, g.reshape(1, d))


def _gluproj_kernel(y_ref, wa_ref, wg_ref, h_ref, g_ref, hnew_ref, hn_ref, wab_ref, wgb_ref):
    @pl.when(pl.program_id(0) == 0)
    def _():
        wab_ref[...] = wa_ref[...].astype(BF16)
        wgb_ref[...] = wg_ref[...].astype(BF16)

    y = y_ref[...].astype(BF16)
    a = jnp.dot(y, wab_ref[...], preferred_element_type=F32)
    gt = jnp.dot(y, wgb_ref[...], preferred_element_type=F32)
    hnew = h_ref[...] + a * jax.nn.sigmoid(gt)
    hnew_ref[...] = hnew
    hn_ref[...] = _rms(hnew, g_ref[...]).astype(hn_ref.dtype)


def gluproj_residual_norm(y, wa, wg, layer, h, g, tm=256):
    s, d = h.shape
    kin = wa.shape[1]
    row = lambda m: (m, 0)
    fixed = lambda m: (0, 0)
    once = lambda m: (layer, 0, 0)
    return pl.pallas_call(
        _gluproj_kernel,
        grid=(s // tm,),
        in_specs=[pl.BlockSpec((tm, kin), row),
                  pl.BlockSpec((None, kin, d), once, pipeline_mode=pl.Buffered(1)),
                  pl.BlockSpec((None, kin, d), once, pipeline_mode=pl.Buffered(1)),
                  pl.BlockSpec((tm, d), row),
                  pl.BlockSpec((1, d), fixed)],
        out_specs=[pl.BlockSpec((tm, d), row), pl.BlockSpec((tm, d), row)],
        out_shape=[jax.ShapeDtypeStruct((s, d), F32), jax.ShapeDtypeStruct((s, d), F32)],
        scratch_shapes=[pltpu.VMEM((kin, d), BF16), pltpu.VMEM((kin, d), BF16)],
        compiler_params=_cparams(("arbitrary",)),
    )(y, wa, wg, h, g.reshape(1, d))


def _swiglu_step(xb, w1_ref, w3_ref, w2_ref):
    a = jnp.dot(xb, w1_ref[...].astype(BF16), preferred_element_type=F32)
    b = jnp.dot(xb, w3_ref[...].astype(BF16), preferred_element_type=F32)
    p = (a * jax.nn.sigmoid(a) * b).astype(BF16)
    return jnp.dot(p, w2_ref[...].astype(BF16), preferred_element_type=F32)


def _ffn_kernel(x_ref, w1_ref, w3_ref, w2_ref, y_ref):
    @pl.when(pl.program_id(1) == 0)
    def _():
        y_ref[...] = jnp.zeros_like(y_ref)

    y_ref[...] += _swiglu_step(x_ref[...], w1_ref, w3_ref, w2_ref)


def ffn(x, w1, w3, w2, layer, tm=1024, tf=256):
    s, d = x.shape
    dff = w1.shape[2]
    row = lambda m, f: (m, 0)
    return pl.pallas_call(
        _ffn_kernel,
        grid=(s // tm, dff // tf),
        in_specs=[pl.BlockSpec((tm, d), row),
                  pl.BlockSpec((None, d, tf), lambda m, f: (layer, 0, f)),
                  pl.BlockSpec((None, d, tf), lambda m, f: (layer, 0, f)),
                  pl.BlockSpec((None, tf, d), lambda m, f: (layer, f, 0))],
        out_specs=pl.BlockSpec((tm, d), row),
        out_shape=jax.ShapeDtypeStruct((s, d), F32),
        compiler_params=_cparams(("parallel", "arbitrary")),
    )(x, w1, w3, w2)


def _ssm_tables(lam_re, lam_im, log_dt, b_re, b_im, c_re, c_im, d_skip, n_doublings):
    L = SSM_CHUNK
    G, P = lam_re.shape
    dt = jnp.exp(log_dt)[:, None]
    mag = jnp.exp(lam_re * dt)
    a_re = mag * jnp.cos(lam_im * dt)
    a_im = mag * jnp.sin(lam_im * dt)
    inv = 1.0 / (lam_re * lam_re + lam_im * lam_im)
    f_re = ((a_re - 1.0) * lam_re + a_im * lam_im) * inv
    f_im = (a_im * lam_re - (a_re - 1.0) * lam_im) * inv
    bb_re = f_re[..., None] * b_re - f_im[..., None] * b_im
    bb_im = f_re[..., None] * b_im + f_im[..., None] * b_re

    pw_re = [jnp.ones_like(a_re)]
    pw_im = [jnp.zeros_like(a_im)]
    for _ in range(L):
        pr, pi = pw_re[-1], pw_im[-1]
        pw_re.append(pr * a_re - pi * a_im)
        pw_im.append(pr * a_im + pi * a_re)
    pw_re = jnp.stack(pw_re)
    pw_im = jnp.stack(pw_im)

    ab_re = pw_re[:L, :, :, None] * bb_re[None] - pw_im[:L, :, :, None] * bb_im[None]
    ab_im = pw_re[:L, :, :, None] * bb_im[None] + pw_im[:L, :, :, None] * bb_re[None]
    hi = lax.Precision.HIGHEST
    kern = (jnp.einsum('gcp,tgpd->tgcd', c_re, ab_re, precision=hi)
            - jnp.einsum('gcp,tgpd->tgcd', c_im, ab_im, precision=hi))
    C = SSM_GROUP
    Q = SSM_LANES // C
    O = G // Q
    kern = kern.at[0].add(jnp.eye(C, dtype=F32)[None] * d_skip.reshape(G, C)[:, :, None])

    kc = kern.reshape(L, O, Q, C, C).transpose(1, 4, 0, 2, 3).reshape(O, C, L * Q * C)

    def by_state(t):
        return jnp.stack(t, axis=3).reshape(O, L, C, 2 * Q * P)

    bc = by_state([ab[::-1].reshape(L, O, Q, P, C).transpose(1, 0, 4, 2, 3) for ab in (ab_re, ab_im)])

    ca_re = c_re[None] * pw_re[1:, :, None, :] - c_im[None] * pw_im[1:, :, None, :]
    ca_im = c_re[None] * pw_im[1:, :, None, :] + c_im[None] * pw_re[1:, :, None, :]
    cc = by_state([ca.reshape(L, O, Q, C, P).transpose(1, 0, 3, 2, 4) for ca in (ca_re, -ca_im)])

    sq_re, sq_im = pw_re[L].reshape(O, Q * P), pw_im[L].reshape(O, Q * P)
    same, cross = [], []
    for _ in range(n_doublings):
        same.append(jnp.concatenate([sq_re, sq_re], axis=-1))
        cross.append(jnp.concatenate([-sq_im, sq_im], axis=-1))
        sq_re, sq_im = sq_re * sq_re - sq_im * sq_im, 2.0 * sq_re * sq_im
    return kc, bc, cc, jnp.stack(same, axis=1), jnp.stack(cross, axis=1)


def _ssm_kernel(u_ref, kc_ref, bc_ref, cc_ref, same_ref, cross_ref, y_ref, yacc, kcat):
    L = SSM_CHUNK
    nc = yacc.shape[0]
    n_state = same_ref.shape[1]
    half = n_state // 2
    groups = SSM_LANES // SSM_GROUP

    def spread(table, col_group):
        rows = lax.broadcasted_iota(jnp.int32, (SSM_LANES, table.shape[1]), 0)
        cols = lax.broadcasted_iota(jnp.int32, (SSM_LANES, table.shape[1]), 1)
        tiled = jnp.concatenate([table] * groups, axis=0)
        return jnp.where(rows // SSM_GROUP == col_group(cols), tiled, 0.0).astype(BF16)

    state_group = lambda cols: (cols % half) // SSM_STATE
    lane_group = lambda cols: (cols % SSM_LANES) // SSM_GROUP

    kcat[...] = spread(kc_ref[...], lane_group)
    yacc[...] = jnp.zeros_like(yacc)
    x = jnp.zeros((nc, n_state), F32)
    for j in range(L):
        uj = u_ref[pl.ds(j, nc, stride=L), :].astype(BF16)
        x = x + jnp.dot(uj, spread(bc_ref[j], state_group), preferred_element_type=F32)
        width = (L - j) * SSM_LANES
        yacc[:, j * SSM_LANES:] += jnp.dot(uj, kcat[:, :width], preferred_element_type=F32)
    row = lax.broadcasted_iota(jnp.int32, x.shape, 0)
    for k in range(same_ref.shape[0]):
        shift = 1 << k
        sh = jnp.where(row >= shift, pltpu.roll(x, shift, 0), 0.0)
        x = x + same_ref[k:k + 1, :] * sh + cross_ref[k:k + 1, :] * pltpu.roll(sh, half, 1)
    prev = jnp.where(row >= 1, pltpu.roll(x, 1, 0), 0.0).astype(BF16)
    for s in range(L):
        carried = lax.dot_general(prev, spread(cc_ref[s], state_group), (((1,), (1,)), ((), ())),
                                  preferred_element_type=F32)
        y_ref[pl.ds(s, nc, stride=L), :] = jax.nn.gelu(yacc[:, s * SSM_LANES:(s + 1) * SSM_LANES] + carried)


def ssm_core(u, params):
    s, width = u.shape
    L = SSM_CHUNK
    nc = s // L
    n_doublings = max(1, (nc - 1).bit_length())
    kc, bc, cc, same, cross = _ssm_tables(*params, n_doublings=n_doublings)
    n_blocks = width // SSM_LANES
    n_state = same.shape[2]
    blk3 = lambda o: (o, 0, 0)
    blk4 = lambda o: (o, 0, 0, 0)
    return pl.pallas_call(
        _ssm_kernel,
        grid=(n_blocks,),
        in_specs=[pl.BlockSpec((s, SSM_LANES), lambda o: (0, o)),
                  pl.BlockSpec((None, SSM_GROUP, L * SSM_LANES), blk3),
                  pl.BlockSpec((None, L, SSM_GROUP, n_state), blk4),
                  pl.BlockSpec((None, L, SSM_GROUP, n_state), blk4),
                  pl.BlockSpec((None, n_doublings, n_state), blk3),
                  pl.BlockSpec((None, n_doublings, n_state), blk3)],
        out_specs=pl.BlockSpec((s, SSM_LANES), lambda o: (0, o)),
        out_shape=jax.ShapeDtypeStruct((s, width), F32),
        scratch_shapes=[pltpu.VMEM((nc, L * SSM_LANES), F32),
                        pltpu.VMEM((SSM_LANES, L * SSM_LANES), BF16)],
        compiler_params=_cparams(("parallel",)),
    )(u, kc, bc, cc, same, cross)


def _router_kernel(x_ref, r_ref, o_ref):
    logits = jnp.dot(x_ref[...], r_ref[...], preferred_element_type=F32, precision=lax.Precision.HIGHEST)
    lane = lax.broadcasted_iota(jnp.int32, logits.shape, 1)
    logits = jnp.where(lane < N_EXPERTS, logits, -jnp.inf)
    big = jnp.int32(logits.shape[1])
    v1 = jnp.max(logits, axis=-1, keepdims=True)
    i1 = jnp.min(jnp.where(logits == v1, lane, big), axis=-1, keepdims=True)
    rest = jnp.where(lane == i1, -jnp.inf, logits)
    v2 = jnp.max(rest, axis=-1, keepdims=True)
    i2 = jnp.min(jnp.where(rest == v2, lane, big), axis=-1, keepdims=True)
    e2 = jnp.exp(v2 - v1)
    g1 = 1.0 / (1.0 + e2)
    g2 = e2 / (1.0 + e2)
    out = jnp.where(lane == 0, i1.astype(F32), 0.0)
    out = jnp.where(lane == 1, i2.astype(F32), out)
    out = jnp.where(lane == 2, g1, out)
    out = jnp.where(lane == 3, g2, out)
    o_ref[...] = out


def router_top2(x, router, tm=512):
    s, d = x.shape
    lanes = 128
    rp = jnp.zeros((d, lanes), F32).at[:, :N_EXPERTS].set(router)
    return pl.pallas_call(
        _router_kernel,
        grid=(s // tm,),
        in_specs=[pl.BlockSpec((tm, d), lambda m: (m, 0)),
                  pl.BlockSpec((d, lanes), lambda m: (0, 0))],
        out_specs=pl.BlockSpec((tm, lanes), lambda m: (m, 0)),
        out_shape=jax.ShapeDtypeStruct((s, lanes), F32),
        compiler_params=_cparams(("parallel",)),
    )(x, rp)


def _moe_kernel(texp_ref, trows_ref, perm_ref,
                x_hbm, w1_ref, w3_ref, w2_ref, y_hbm, xb, acc, gsem, ssem):
    t = pl.program_id(0)
    f = pl.program_id(1)
    tm = acc.shape[0]
    n_tok = x_hbm.shape[0]
    rows = trows_ref[t]
    base = t * tm

    def gather_copy(i):
        a = perm_ref[base + i]
        tok = jnp.where(a >= n_tok, a - n_tok, a)
        return pltpu.make_async_copy(x_hbm.at[pl.ds(tok, 1), :], acc.at[pl.ds(i, 1), :], gsem)

    def scatter_copy(i):
        return pltpu.make_async_copy(acc.at[pl.ds(i, 1), :], y_hbm.at[pl.ds(perm_ref[base + i], 1), :], ssem)

    def for_rows(fn):
        groups = lax.shift_right_logical(rows, DMA_UNROLL.bit_length() - 1)

        def group_body(gi, carry):
            for u in range(DMA_UNROLL):
                fn(gi * DMA_UNROLL + u, u)
            return carry

        def row_body(i, carry):
            fn(i, 0)
            return carry

        lax.fori_loop(0, groups, group_body, 0)
        lax.fori_loop(groups * DMA_UNROLL, rows, row_body, 0)

    def start_all(make):
        for_rows(lambda i, u: make(i).start(priority=u % 2))

    def wait_all(make):
        for_rows(lambda i, u: make(i).wait())

    @pl.when(rows > 0)
    def _():
        @pl.when(f == 0)
        def _():
            acc[...] = jnp.zeros_like(acc)
            start_all(gather_copy)
            wait_all(gather_copy)
            xb[...] = acc[...].astype(BF16)
            acc[...] = jnp.zeros_like(acc)

        lo = 0
        for m in _moe_row_buckets(tm):
            @pl.when((rows > lo) & (rows <= m))
            def _(m=m):
                acc[0:m, :] += _swiglu_step(xb[0:m, :], w1_ref, w3_ref, w2_ref)
            lo = m

        @pl.when(f == pl.num_programs(1) - 1)
        def _():
            start_all(scatter_copy)
            wait_all(scatter_copy)


def _moe_row_buckets(tm):
    return tuple(range(2 * MOE_ROW_STEP, tm, MOE_ROW_STEP)) + (tm,)


def moe_experts(x, w1, w3, w2, layer, tile_expert, tile_rows, perm, tm, tf=256):
    s, d = x.shape
    dff = w1.shape[3]
    n_tiles = tile_expert.shape[0]
    nf = dff // tf
    last = nf - 1

    def fsel(t, f, trows):
        return jnp.where(trows[t] > 0, f, last)

    grid_spec = pltpu.PrefetchScalarGridSpec(
        num_scalar_prefetch=3,
        grid=(n_tiles, nf),
        in_specs=[pl.BlockSpec(memory_space=pl.ANY),
                  pl.BlockSpec((None, None, d, tf), lambda t, f, te, tr, pm: (layer, te[t], 0, fsel(t, f, tr))),
                  pl.BlockSpec((None, None, d, tf), lambda t, f, te, tr, pm: (layer, te[t], 0, fsel(t, f, tr))),
                  pl.BlockSpec((None, None, tf, d), lambda t, f, te, tr, pm: (layer, te[t], fsel(t, f, tr), 0))],
        out_specs=pl.BlockSpec(memory_space=pl.ANY),
        scratch_shapes=[pltpu.VMEM((tm, d), BF16),
                        pltpu.VMEM((tm, d), F32),
                        pltpu.SemaphoreType.DMA(()),
                        pltpu.SemaphoreType.DMA(())],
    )
    return pl.pallas_call(
        _moe_kernel,
        grid_spec=grid_spec,
        out_shape=jax.ShapeDtypeStruct((TOP_K * s, d), F32),
        compiler_params=_cparams(("arbitrary", "arbitrary")),
    )(tile_expert, tile_rows, perm, x, w1, w3, w2)


def _route_plan(e1, e2, tm, n_tiles):
    s = e1.shape[0]
    experts = jnp.concatenate([e1, e2])
    onehot = (experts[:, None] == jnp.arange(N_EXPERTS, dtype=jnp.int32)[None, :]).astype(jnp.int32)
    csum = jnp.cumsum(onehot, axis=0)
    counts = csum[-1]
    tiles_per = (counts + tm - 1) // tm
    tile_end = jnp.cumsum(tiles_per)
    tile_start = tile_end - tiles_per
    per_tile = (counts + jnp.maximum(tiles_per, 1) - 1) // jnp.maximum(tiles_per, 1)
    rank = jnp.sum(onehot * (csum - 1), axis=1)
    own_start = jnp.sum(onehot * tile_start[None, :], axis=1)
    own_per_tile = jnp.maximum(jnp.sum(onehot * per_tile[None, :], axis=1), 1)
    pos = (own_start + rank // own_per_tile) * tm + rank % own_per_tile
    n_rows = n_tiles * tm
    perm = jnp.zeros((n_rows,), jnp.int32).at[pos].set(jnp.arange(TOP_K * s, dtype=jnp.int32))
    tid = jnp.arange(n_tiles, dtype=jnp.int32)
    texp = jnp.minimum(jnp.sum((tid[:, None] >= tile_end[None, :]).astype(jnp.int32), axis=1), N_EXPERTS - 1)
    used = tid < tile_end[-1]
    local = tid - tile_start[texp]
    trows = jnp.where(used, jnp.clip(counts[texp] - local * per_tile[texp], 0, per_tile[texp]), 0).astype(jnp.int32)
    last_used = texp[jnp.maximum(tile_end[-1] - 1, 0)]
    texp = jnp.where(used, texp, last_used).astype(jnp.int32)
    return texp, trows, perm


def _combine_kernel(h_ref, y1_ref, y2_ref, gt_ref, g_ref, *out_refs, want_h):
    gates = gt_ref[...]
    hnew = h_ref[...] + gates[:, 2:3] * y1_ref[...] + gates[:, 3:4] * y2_ref[...]
    if want_h:
        out_refs[0][...] = hnew
    out_refs[-1][...] = _rms(hnew, g_ref[...]).astype(out_refs[-1].dtype)


def combine_residual_norm(h, y, route, g, want_h, norm_dtype, tm=512):
    s, d = h.shape
    nb = s // tm
    row = lambda m: (m, 0)
    outs_spec = [pl.BlockSpec((tm, d), row)]
    outs_shape = [jax.ShapeDtypeStruct((s, d), norm_dtype)]
    if want_h:
        outs_spec = [pl.BlockSpec((tm, d), row)] + outs_spec
        outs_shape = [jax.ShapeDtypeStruct((s, d), F32)] + outs_shape
    return pl.pallas_call(
        functools.partial(_combine_kernel, want_h=want_h),
        grid=(nb,),
        in_specs=[pl.BlockSpec((tm, d), row),
                  pl.BlockSpec((tm, d), row),
                  pl.BlockSpec((tm, d), lambda m: (m + nb, 0)),
                  pl.BlockSpec((tm, route.shape[1]), row),
                  pl.BlockSpec((1, d), lambda m: (0, 0))],
        out_specs=outs_spec,
        out_shape=outs_shape,
        compiler_params=_cparams(("parallel",)),
    )(h, y, y, route, g.reshape(1, d))


MOE_TM = 2176


def kernel(x, even_norm1, even_w_in, even_conv_w, even_conv_b, even_cln_g, even_cln_b, even_w_out, even_norm2, even_ffn_w1, even_ffn_w3, even_ffn_w2, odd_norm1, odd_ssm_w_in, odd_lam_re, odd_lam_im, odd_log_dt, odd_b_re, odd_b_im, odd_c_re, odd_c_im, odd_d_skip, odd_glu_wa, odd_glu_wg, odd_norm2, odd_router, odd_moe_w1, odd_moe_w3, odd_moe_w2, final_norm):
    b, s, d = x.shape
    assert b == 1
    depth = even_norm1.shape[0] + odd_norm1.shape[0]
    assert even_norm1.shape[0] == odd_norm1.shape[0]
    h = x.reshape(s, d)
    hn = rmsnorm(h, even_norm1[0], BF16)
    n_tiles = (TOP_K * s) // MOE_TM + N_EXPERTS
    out = None
    ffn_out = None
    for layer in range(depth):
        j = layer // 2
        if layer % 2 == 0:
            proj = matmul(hn, even_w_in, j)
            att = dilated_attention(proj)
            cv = conv_module(proj, even_conv_w[j], even_conv_b[j], even_cln_g[j], even_cln_b[j])
            h, hn = outproj_residual_norm(att, cv, even_w_out, j, h, even_norm2[j])
            ffn_out = ffn(hn, even_ffn_w1, even_ffn_w3, even_ffn_w2, j)
        else:
            h, u = residual_norm_matmul(h, ffn_out, odd_norm1[j], odd_ssm_w_in, j)
            y = ssm_core(u, (odd_lam_re[j], odd_lam_im[j], odd_log_dt[j], odd_b_re[j], odd_b_im[j],
                             odd_c_re[j], odd_c_im[j], odd_d_skip[j]))
            h, hn = gluproj_residual_norm(y, odd_glu_wa, odd_glu_wg, j, h, odd_norm2[j])
            route = router_top2(hn, odd_router[j])
            e1 = route[:, 0].astype(jnp.int32)
            e2 = route[:, 1].astype(jnp.int32)
            texp, trows, perm = _route_plan(e1, e2, MOE_TM, n_tiles)
            ys = moe_experts(hn, odd_moe_w1, odd_moe_w3, odd_moe_w2, j, texp, trows, perm, MOE_TM)
            if layer + 1 < depth:
                h, hn = combine_residual_norm(h, ys, route, even_norm1[j + 1], True, BF16)
            else:
                (out,) = combine_residual_norm(h, ys, route, final_norm, False, F32)
    return out.reshape(b, s, d)
```
